```python
import math
import functools
import jax
import jax.numpy as jnp
from jax import lax
import numpy as np

D_MODEL = 2048
BATCH = 4
SEQ = 4096
DEPTH = 2

GRID_W = 64
CTX_LEN = 256
HEAD_DIM = 128
NA_HEADS = 3 * D_MODEL // (8 * HEAD_DIM)
NA_WIN_R = 8
NA_WIN_C = 16
SW_HEADS = 3 * D_MODEL // (8 * HEAD_DIM)
SW_KV_HEADS = SW_HEADS // 3
SW_WINDOW = 128
SW_BLOCK = 128
DF_V_DIM = 128
DF_QK_DIM = DF_V_DIM // 2
DF_HEADS = D_MODEL // (4 * DF_V_DIM)
DF_BLOCK = 128
MIX_WIDTH = NA_HEADS * HEAD_DIM + SW_HEADS * HEAD_DIM + DF_HEADS * DF_V_DIM
IN_WIDTHS = ((NA_HEADS * HEAD_DIM,) * 3
             + (SW_HEADS * HEAD_DIM, SW_KV_HEADS * HEAD_DIM, SW_KV_HEADS * HEAD_DIM)
             + (2 * DF_HEADS * DF_QK_DIM, 2 * DF_HEADS * DF_QK_DIM, DF_HEADS * DF_V_DIM))
IN_WIDTH = sum(IN_WIDTHS)
D_FF = 5632
N_EXPERTS = 8
TOP_K = 2
ROPE_THETA = 10000.0
NORM_EPS = 1e-6

kernel_name = 'hybrid_natten_swa_diff_moe_dit'


def rms_norm(x, g):
    xf = x.astype(jnp.float32)
    y = xf * lax.rsqrt(jnp.mean(xf * xf, axis=-1, keepdims=True) + NORM_EPS)
    return (y * g.astype(jnp.float32)).astype(x.dtype)


def modulate(h, shift, scale):
    return h * (1.0 + scale) + shift


def axial_rope_tables(n_tokens, head_dim):
    axis_dim = head_dim // 2
    inv_freq = ROPE_THETA ** (-jnp.arange(0, axis_dim, 2, dtype=jnp.float32) / axis_dim)
    t = jnp.arange(n_tokens, dtype=jnp.int32)
    row = (t // GRID_W).astype(jnp.float32)
    col = (t % GRID_W).astype(jnp.float32)
    ang_r = row[:, None] * inv_freq[None, :]
    ang_c = col[:, None] * inv_freq[None, :]
    return (jnp.cos(ang_r), jnp.sin(ang_r), jnp.cos(ang_c), jnp.sin(ang_c))


def _rotate_half(x, cos, sin):
    x1, x2 = jnp.split(x, 2, axis=-1)
    return jnp.concatenate([x1 * cos - x2 * sin, x2 * cos + x1 * sin], axis=-1)


def apply_axial_rope(x, tables):
    cos_r, sin_r, cos_c, sin_c = tables
    L = x.shape[1]
    shp = (L,) + (1,) * (x.ndim - 3) + (cos_r.shape[-1],)
    xr, xc = jnp.split(x.astype(jnp.float32), 2, axis=-1)
    out = jnp.concatenate([_rotate_half(xr, cos_r.reshape(shp), sin_r.reshape(shp)),
                           _rotate_half(xc, cos_c.reshape(shp), sin_c.reshape(shp))], axis=-1)
    return out.astype(x.dtype)


def project_heads(h, w_in):
    B, L, _ = h.shape
    p = h @ w_in
    offs = [0]
    for wdt in IN_WIDTHS:
        offs.append(offs[-1] + wdt)
    shapes = (((NA_HEADS, HEAD_DIM),) * 3
              + ((SW_HEADS, HEAD_DIM), (SW_KV_HEADS, HEAD_DIM), (SW_KV_HEADS, HEAD_DIM))
              + ((DF_HEADS, 2, DF_QK_DIM), (DF_HEADS, 2, DF_QK_DIM), (DF_HEADS, DF_V_DIM)))
    return tuple(p[..., offs[j]:offs[j + 1]].reshape((B, L) + shapes[j]) for j in range(len(shapes)))


def prepare_heads(h, w_in, norms, rope_b, rope_c):
    na_qg, na_kg, sw_qg, sw_kg, df_qg, df_kg = norms
    qa, ka, va, qb, kb, vb, qc, kc, vc = project_heads(h, w_in)
    qa, ka = rms_norm(qa, na_qg), rms_norm(ka, na_kg)
    qb, kb = rms_norm(qb, sw_qg), rms_norm(kb, sw_kg)
    qc, kc = rms_norm(qc, df_qg), rms_norm(kc, df_kg)
    if rope_b is not None:
        qb, kb = apply_axial_rope(qb, rope_b), apply_axial_rope(kb, rope_b)
        qc, kc = apply_axial_rope(qc, rope_c), apply_axial_rope(kc, rope_c)
    return qa, ka, va, qb, kb, vb, qc, kc, vc


def neighbourhood_attention(q, k, v, k_ctx, v_ctx, rpb):
    B, L, H, d = q.shape
    rows = L // GRID_W
    wr = min(NA_WIN_R, rows)
    wc = min(NA_WIN_C, GRID_W)
    scale = d ** -0.5
    qg = q.reshape(B, rows, GRID_W, H, d)
    kg = k.reshape(B, rows, GRID_W, H, d)
    vg = v.reshape(B, rows, GRID_W, H, d)
    col = jnp.arange(GRID_W)
    col_idx = jnp.clip(col - wc // 2, 0, GRID_W - wc)[:, None] + jnp.arange(wc)[None, :]
    ic = (col_idx - col[:, None] + NA_WIN_C - 1)[None]

    def one_row(r):
        r0 = jnp.clip(r - wr // 2, 0, rows - wr)
        q_r = lax.dynamic_index_in_dim(qg, r, axis=1, keepdims=False)
        k_nb = lax.dynamic_slice_in_dim(kg, r0, wr, axis=1)[:, :, col_idx]
        v_nb = lax.dynamic_slice_in_dim(vg, r0, wr, axis=1)[:, :, col_idx]
        ir = (r0 + jnp.arange(wr) - r + NA_WIN_R - 1)[:, None, None]
        bias = jnp.transpose(rpb[:, ir, ic], (0, 2, 1, 3)).astype(jnp.float32)
        s_nb = jnp.einsum('bqhd,brqkhd->bhqrk', q_r, k_nb).astype(jnp.float32) * scale + bias[None]
        s_cx = jnp.einsum('bqhd,bkhd->bhqk', q_r, k_ctx).astype(jnp.float32) * scale
        logits = jnp.concatenate([s_nb.reshape(B, H, GRID_W, wr * wc), s_cx], axis=-1)
        p = jax.nn.softmax(logits, axis=-1).astype(v.dtype)
        p_nb = p[..., :wr * wc].reshape(B, H, GRID_W, wr, wc)
        return (jnp.einsum('bhqrk,brqkhd->bqhd', p_nb, v_nb)
                + jnp.einsum('bhqk,bkhd->bqhd', p[..., wr * wc:], v_ctx))

    o = lax.map(one_row, jnp.arange(rows))
    return jnp.moveaxis(o, 0, 1).reshape(B, L, H, d)


def window_attention(q, k, v, k_ctx, v_ctx, sink):
    B, L, Hq, d = q.shape
    Hkv = k.shape[2]
    G = Hq // Hkv
    nb = L // SW_BLOCK
    Lc = k_ctx.shape[1]
    scale = d ** -0.5
    qb = q.reshape(B, nb, SW_BLOCK, Hkv, G, d)

    def band(t):
        pad = jnp.zeros((B, SW_BLOCK) + t.shape[2:], t.dtype)
        tp = jnp.concatenate([pad, t, pad], axis=1).reshape((B, nb + 2, SW_BLOCK) + t.shape[2:])
        return jnp.concatenate([tp[:, :-2], tp[:, 1:-1], tp[:, 2:]], axis=2)

    kb, vb = band(k), band(v)
    blk = jnp.arange(nb)[:, None]
    q_pos = blk * SW_BLOCK + jnp.arange(SW_BLOCK)[None, :]
    k_pos = (blk - 1) * SW_BLOCK + jnp.arange(3 * SW_BLOCK)[None, :]
    valid = (((k_pos >= 0) & (k_pos < L))[:, None, :]
             & (jnp.abs(q_pos[:, :, None] - k_pos[:, None, :]) <= SW_WINDOW))
    s_win = jnp.einsum('bnqhgd,bnkhd->bhgnqk', qb, kb).astype(jnp.float32) * scale
    s_win = jnp.where(valid, s_win, -jnp.inf)
    s_ctx = jnp.einsum('bnqhgd,bkhd->bhgnqk', qb, k_ctx).astype(jnp.float32) * scale
    s_sink = jnp.broadcast_to(sink.astype(jnp.float32).reshape(1, Hkv, G, 1, 1, 1), s_win.shape[:-1] + (1,))
    p = jax.nn.softmax(jnp.concatenate([s_win, s_ctx, s_sink], axis=-1), axis=-1).astype(v.dtype)
    nw = 3 * SW_BLOCK
    out = (jnp.einsum('bhgnqk,bnkhd->bnqhgd', p[..., :nw], vb)
           + jnp.einsum('bhgnqk,bkhd->bnqhgd', p[..., nw:nw + Lc], v_ctx))
    return out.reshape(B, L, Hq, d)


def context_attention(q, k, v, sink):
    B, Lc, Hq, d = q.shape
    Hkv = k.shape[2]
    G = Hq // Hkv
    qg = q.reshape(B, Lc, Hkv, G, d)
    s = jnp.einsum('bqhgd,bkhd->bhgqk', qg, k).astype(jnp.float32) * d ** -0.5
    if sink is not None:
        s_sink = jnp.broadcast_to(sink.astype(jnp.float32).reshape(1, Hkv, G, 1, 1), s.shape[:-1] + (1,))
        s = jnp.concatenate([s, s_sink], axis=-1)
    p = jax.nn.softmax(s, axis=-1)[..., :Lc].astype(v.dtype)
    return jnp.einsum('bhgqk,bkhd->bqhgd', p, v).reshape(B, Lc, Hq, d)


def diff_attention(q, k_all, v_all, lam):
    B, L, H, _, dk = q.shape
    nb = L // DF_BLOCK
    scale = dk ** -0.5
    qb = jnp.moveaxis(q.reshape(B, nb, DF_BLOCK, H, 2, dk), 1, 0)

    def one_block(q_blk):
        s = jnp.einsum('bqhtd,bkhtd->bhtqk', q_blk, k_all).astype(jnp.float32) * scale
        p = jax.nn.softmax(s, axis=-1)
        w = p[:, :, 0] - lam * p[:, :, 1]
        return jnp.einsum('bhqk,bkhd->bqhd', w.astype(v_all.dtype), v_all)

    o = lax.map(one_block, qb)
    return jnp.moveaxis(o, 0, 1).reshape(B, L, H, v_all.shape[-1])


def merge_head_groups(oa, ob, oc, w_out):
    B, L = oa.shape[:2]
    return jnp.concatenate([oa.reshape(B, L, -1), ob.reshape(B, L, -1), oc.reshape(B, L, -1)], axis=-1) @ w_out


def swiglu(h, w1, w3, w2):
    return (jax.nn.silu(h @ w1) * (h @ w3)) @ w2


def moe_swiglu(h, router, w1, w3, w2):
    logits = (h @ router).astype(jnp.float32)
    top_v, top_i = lax.top_k(logits, TOP_K)
    top_p = jax.nn.softmax(top_v, axis=-1)
    gates = jnp.sum(jax.nn.one_hot(top_i, N_EXPERTS, dtype=jnp.float32) * top_p[..., None], axis=-2).astype(h.dtype)
    y = jnp.zeros_like(h)
    for e in range(N_EXPERTS):
        y = y + gates[..., e:e + 1] * swiglu(h, w1[e], w3[e], w2[e])
    return y


def setup_inputs(seed: int = 0) -> dict:
    key = jax.random.key(seed)
    ks = jax.random.split(key, 27)
    f32 = jnp.float32
    n_dense = (DEPTH + 1) // 2
    n_moe = DEPTH // 2

    def nrm(k, shape, scale):
        return jax.random.normal(k, shape, f32) * scale

    def gain(k, shape):
        return 1.0 + nrm(k, shape, 0.05)

    return {
        'x': nrm(ks[0], (BATCH, SEQ, D_MODEL), 1.0),
        'c': nrm(ks[1], (BATCH, D_MODEL), 1.0),
        'ctx': nrm(ks[2], (BATCH, CTX_LEN, D_MODEL), 1.0),
        'c_ctx': nrm(ks[3], (D_MODEL,), 1.0),
        'ada_w': nrm(ks[4], (DEPTH, D_MODEL, 6 * D_MODEL), 0.5 * D_MODEL ** -0.5),
        'ada_b': nrm(ks[5], (DEPTH, 6 * D_MODEL), 0.02),
        'norm1_g': gain(ks[6], (DEPTH, D_MODEL)),
        'norm2_g': gain(ks[7], (DEPTH, D_MODEL)),
        'w_in': nrm(ks[8], (DEPTH, D_MODEL, IN_WIDTH), D_MODEL ** -0.5),
        'w_out': nrm(ks[9], (DEPTH, MIX_WIDTH, D_MODEL), MIX_WIDTH ** -0.5),
        'na_q_norm': gain(ks[10], (DEPTH, HEAD_DIM)),
        'na_k_norm': gain(ks[11], (DEPTH, HEAD_DIM)),
        'na_rpb': nrm(ks[12], (DEPTH, NA_HEADS, 2 * NA_WIN_R - 1, 2 * NA_WIN_C - 1), 0.1),
        'sw_q_norm': gain(ks[13], (DEPTH, HEAD_DIM)),
        'sw_k_norm': gain(ks[14], (DEPTH, HEAD_DIM)),
        'sw_sink': nrm(ks[15], (DEPTH, SW_HEADS), 0.5),
        'df_q_norm': gain(ks[16], (DEPTH, DF_QK_DIM)),
        'df_k_norm': gain(ks[17], (DEPTH, DF_QK_DIM)),
        'df_lambda': nrm(ks[18], (DEPTH, 4, DF_QK_DIM), 0.1),
        'df_subln_g': gain(ks[19], (DEPTH, DF_V_DIM)),
        'ffn_w1': nrm(ks[20], (n_dense, D_MODEL, D_FF), D_MODEL ** -0.5),
        'ffn_w3': nrm(ks[21], (n_dense, D_MODEL, D_FF), D_MODEL ** -0.5),
        'ffn_w2': nrm(ks[22], (n_dense, D_FF, D_MODEL), D_FF ** -0.5),
        'moe_router': nrm(ks[23], (n_moe, D_MODEL, N_EXPERTS), D_MODEL ** -0.5),
        'moe_w1': nrm(ks[24], (n_moe, N_EXPERTS, D_MODEL, D_FF), D_MODEL ** -0.5),
        'moe_w3': nrm(ks[25], (n_moe, N_EXPERTS, D_MODEL, D_FF), D_MODEL ** -0.5),
        'moe_w2': nrm(ks[26], (n_moe, N_EXPERTS, D_FF, D_MODEL), D_FF ** -0.5),
    }


def reference(x, c, ctx, c_ctx, ada_w, ada_b, norm1_g, norm2_g, w_in, w_out,
              na_q_norm, na_k_norm, na_rpb, sw_q_norm, sw_k_norm, sw_sink,
              df_q_norm, df_k_norm, df_lambda, df_subln_g,
              ffn_w1, ffn_w3, ffn_w2, moe_router, moe_w1, moe_w3, moe_w2):
    L = x.shape[1]
    rope_b = axial_rope_tables(L, HEAD_DIM)
    rope_c = axial_rope_tables(L, DF_QK_DIM)
    silu_c = jax.nn.silu(c)
    silu_cc = jax.nn.silu(c_ctx)[None]
    xc = ctx
    for i in range(DEPTH):
        last = i == DEPTH - 1
        mx = [m[:, None, :] for m in jnp.split(silu_c @ ada_w[i] + ada_b[i], 6, axis=-1)]
        mc = [m[:, None, :] for m in jnp.split(silu_cc @ ada_w[i] + ada_b[i], 6, axis=-1)]
        norms = (na_q_norm[i], na_k_norm[i], sw_q_norm[i], sw_k_norm[i], df_q_norm[i], df_k_norm[i])
        lq1, lk1, lq2, lk2 = df_lambda[i].astype(jnp.float32)
        lam_init = 0.8 - 0.6 * math.exp(-0.3 * i)
        lam = jnp.exp(jnp.sum(lq1 * lk1)) - jnp.exp(jnp.sum(lq2 * lk2)) + lam_init

        hx = modulate(rms_norm(x, norm1_g[i]), mx[0], mx[1])
        hc = modulate(rms_norm(xc, norm1_g[i]), mc[0], mc[1])
        qa, ka, va, qb, kb, vb, qc, kc, vc = prepare_heads(hx, w_in[i], norms, rope_b, rope_c)
        cqa, cka, cva, cqb, ckb, cvb, cqc, ckc, cvc = prepare_heads(hc, w_in[i], norms, None, None)
        oa = neighbourhood_attention(qa, ka, va, cka, cva, na_rpb[i])
        ob = window_attention(qb, kb, vb, ckb, cvb, sw_sink[i])
        oc = diff_attention(qc, jnp.concatenate([kc, ckc], axis=1), jnp.concatenate([vc, cvc], axis=1), lam)
        oc = rms_norm(oc, df_subln_g[i]) * (1.0 - lam_init)
        x = x + mx[2] * merge_head_groups(oa, ob, oc, w_out[i])

        j = i // 2
        if i % 2 == 0:
            ffn = functools.partial(swiglu, w1=ffn_w1[j], w3=ffn_w3[j], w2=ffn_w2[j])
        else:
            ffn = functools.partial(moe_swiglu, router=moe_router[j], w1=moe_w1[j], w3=moe_w3[j], w2=moe_w2[j])

        if not last:
            coa = context_attention(cqa, cka, cva, None)
            cob = context_attention(cqb, ckb, cvb, sw_sink[i])
            coc = rms_norm(diff_attention(cqc, ckc, cvc, lam), df_subln_g[i]) * (1.0 - lam_init)
            xc = xc + mc[2] * merge_head_groups(coa, cob, coc, w_out[i])
            xc = xc + mc[5] * ffn(modulate(rms_norm(xc, norm2_g[i]), mc[3], mc[4]))
        x = x + mx[5] * ffn(modulate(rms_norm(x, norm2_g[i]), mx[3], mx[4]))
    return x
```

```python
import functools
import math

import numpy as np
import jax
import jax.numpy as jnp
from jax import lax
from jax.experimental import pallas as pl
from jax.experimental.pallas import tpu as pltpu

F32 = jnp.float32
BF16 = jnp.bfloat16

D_MODEL = 2048
GRID_W = 64
HEAD_DIM = 128
NA_HEADS = 6
NA_WIN_R = 8
NA_WIN_C = 16
SW_HEADS = 6
SW_KV_HEADS = 2
SW_GROUP = SW_HEADS // SW_KV_HEADS
SW_WINDOW = 128
DF_HEADS = 4
DF_QK_DIM = 64
DF_V_DIM = 128
IN_WIDTH = 5120
D_FF = 5632
N_EXPERTS = 8
ROPE_THETA = 10000.0
NORM_EPS = 1e-6
NEG = -1e30

LANES = 128
VMEM_LIMIT = 56 * 1024 * 1024

COL_QA, COL_KA, COL_VA = 0, 6, 12
COL_QB, COL_KB, COL_VB = 18, 24, 26
COL_QC, COL_KC, COL_VC = 28, 32, 36

T_PLAIN, T_NORM, T_NORM_ROPE_B, T_NORM_ROPE_C = 0, 1, 2, 3
CHUNK_TYPES = np.array([T_NORM] * 6 + [T_PLAIN] * 3 + [T_NORM_ROPE_B] * 4 + [T_PLAIN]
                       + [T_NORM_ROPE_C] * 4 + [T_PLAIN] * 2, dtype=np.int32)

NA_QROWS = 4
NA_KROWS = 12
NA_TQ = NA_QROWS * GRID_W
NA_TK = NA_KROWS * GRID_W
SW_TQ = 256
SW_TK = SW_TQ + 2 * SW_WINDOW
DF_TQ = 256
DF_TK = 512


def _cparams(sem):
    return pltpu.CompilerParams(dimension_semantics=sem, vmem_limit_bytes=VMEM_LIMIT)


def _dot(a, b):
    return jnp.dot(a, b, preferred_element_type=F32)


def _dot_nt(a, b):
    return lax.dot_general(a, b, (((1,), (1,)), ((), ())), preferred_element_type=F32)


def _ada_kernel(c_ref, w_ref, b_ref, o_ref):
    c = c_ref[...]
    sc = (c * jax.nn.sigmoid(c)).astype(BF16)
    o_ref[0] = _dot(sc, w_ref[0].astype(BF16)) + b_ref[0]


def _ada_call(c_rows, ada_w, ada_b):
    depth, d, n = ada_w.shape
    rows = c_rows.shape[0]
    tn = 1024
    return pl.pallas_call(
        _ada_kernel,
        grid=(depth, n // tn),
        in_specs=[pl.BlockSpec((rows, d), lambda l, j: (0, 0)),
                  pl.BlockSpec((1, d, tn), lambda l, j: (l, 0, j)),
                  pl.BlockSpec((1, 1, tn), lambda l, j: (l, 0, j))],
        out_specs=pl.BlockSpec((1, rows, tn), lambda l, j: (l, 0, j)),
        out_shape=jax.ShapeDtypeStruct((depth, rows, n), F32),
        compiler_params=_cparams(("arbitrary", "arbitrary")),
        name="ada",
    )(c_rows, ada_w, ada_b.reshape(depth, 1, n))


def _rms(v, gain):
    ms = jnp.mean(v * v, axis=-1, keepdims=True)
    return v * lax.rsqrt(ms + NORM_EPS) * gain


def _rms_half(v, gain, lane):
    lo = lane < DF_QK_DIM
    sq = v * v
    s_lo = jnp.sum(jnp.where(lo, sq, 0.0), axis=-1, keepdims=True)
    s_hi = jnp.sum(jnp.where(lo, 0.0, sq), axis=-1, keepdims=True)
    ms = jnp.where(lo, s_lo, s_hi) * (1.0 / DF_QK_DIM)
    return v * lax.rsqrt(ms + NORM_EPS) * gain


def _rope(v, cos, sin, half, lane):
    fwd = pltpu.roll(v, LANES - half, axis=1)
    bwd = pltpu.roll(v, half, axis=1)
    partner = jnp.where((lane & half) == 0, fwd, bwd)
    return v * cos + partner * sin


def _proj_kernel(types_ref, x_ref, shift_ref, scale_ref, g_ref, w_ref, gain_ref,
                 cb_ref, sb_ref, cc_ref, sc_ref, o_ref, h_scr, *, nch):
    j = pl.program_id(1)

    @pl.when(j == 0)
    def _():
        x = x_ref[...]
        h = _rms(x, g_ref[...]) * (1.0 + scale_ref[0]) + shift_ref[0]
        h_scr[...] = h.astype(BF16)

    p = _dot(h_scr[...], w_ref[...])
    tm = p.shape[0]
    lane = lax.broadcasted_iota(jnp.int32, (tm, LANES), 1)
    for s in range(nch):
        t = types_ref[j * nch + s]
        for hh in range(2):
            c0 = s * 256 + hh * LANES
            v = p[:, c0:c0 + LANES]
            gain = gain_ref[:, c0:c0 + LANES]

            @pl.when(t == T_PLAIN)
            def _():
                o_ref[:, c0:c0 + LANES] = v.astype(BF16)

            @pl.when(t == T_NORM)
            def _():
                o_ref[:, c0:c0 + LANES] = _rms(v, gain).astype(BF16)

            @pl.when(t == T_NORM_ROPE_B)
            def _():
                y = _rope(_rms(v, gain), cb_ref[...], sb_ref[...], HEAD_DIM // 4, lane)
                o_ref[:, c0:c0 + LANES] = y.astype(BF16)

            @pl.when(t == T_NORM_ROPE_C)
            def _():
                y = _rope(_rms_half(v, gain, lane), cc_ref[...], sc_ref[...], DF_QK_DIM // 4, lane)
                o_ref[:, c0:c0 + LANES] = y.astype(BF16)


def _proj_call(x2d, shift, scale, g, w, gains, tabs, seq, tm, tn=512):
    t_rows, d = x2d.shape
    n = w.shape[1]
    tpb = seq // tm
    nch = tn // 256
    grid_spec = pltpu.PrefetchScalarGridSpec(
        num_scalar_prefetch=1,
        grid=(t_rows // tm, n // tn),
        in_specs=[pl.BlockSpec((tm, d), lambda i, j, t: (i, 0)),
                  pl.BlockSpec((1, 1, d), lambda i, j, t: (i // tpb, 0, 0)),
                  pl.BlockSpec((1, 1, d), lambda i, j, t: (i // tpb, 0, 0)),
                  pl.BlockSpec((1, d), lambda i, j, t: (0, 0)),
                  pl.BlockSpec((d, tn), lambda i, j, t: (0, j)),
                  pl.BlockSpec((1, tn), lambda i, j, t: (0, j))]
                 + [pl.BlockSpec((tm, LANES), lambda i, j, t: (i % tpb, 0))] * 4,
        out_specs=pl.BlockSpec((tm, tn), lambda i, j, t: (i, j)),
        scratch_shapes=[pltpu.VMEM((tm, d), BF16)])
    return pl.pallas_call(
        functools.partial(_proj_kernel, nch=nch),
        grid_spec=grid_spec,
        out_shape=jax.ShapeDtypeStruct((t_rows, n), BF16),
        compiler_params=_cparams(("arbitrary", "arbitrary")),
        name="proj",
    )(jnp.asarray(CHUNK_TYPES), x2d, shift, scale, g, w, gains, *tabs)


def _softmax_parts(parts):
    m = parts[0].max(axis=-1, keepdims=True)
    for s in parts[1:]:
        m = jnp.maximum(m, s.max(axis=-1, keepdims=True))
    return m


def _na_kernel(q_ref, k_ref, v_ref, kc_ref, vc_ref, bias_ref, o_ref, *, nblk):
    blk = pl.program_id(2)
    u0 = jnp.clip(NA_QROWS * blk - NA_WIN_R // 2, 0, GRID_W - NA_KROWS)
    start = pl.multiple_of(u0 * GRID_W, GRID_W)
    var = jnp.where(blk == 0, 0, jnp.where(blk == nblk - 1, 2, 1))
    scale = HEAD_DIM ** -0.5
    q = q_ref[0]
    kw = k_ref[0, pl.ds(start, NA_TK), :]
    vw = v_ref[0, pl.ds(start, NA_TK), :]
    s_nb = _dot_nt(q, kw) * scale + bias_ref[0, var]
    s_cx = _dot_nt(q, kc_ref[0]) * scale
    m = jnp.maximum(s_nb.max(axis=-1, keepdims=True), s_cx.max(axis=-1, keepdims=True))
    p_nb = jnp.exp(s_nb - m)
    p_cx = jnp.exp(s_cx - m)
    l = p_nb.sum(axis=-1, keepdims=True) + p_cx.sum(axis=-1, keepdims=True)
    o = _dot(p_nb.astype(BF16), vw) + _dot(p_cx.astype(BF16), vc_ref[0])
    o_ref[0] = (o / l).astype(BF16)


def _na_call(p, pc, bias):
    b, seq, _ = p.shape
    lc = pc.shape[1]
    nblk = seq // NA_TQ
    return pl.pallas_call(
        functools.partial(_na_kernel, nblk=nblk),
        grid=(b, NA_HEADS, nblk),
        in_specs=[pl.BlockSpec((1, NA_TQ, LANES), lambda bi, h, i: (bi, i, COL_QA + h)),
                  pl.BlockSpec((1, seq, LANES), lambda bi, h, i: (bi, 0, COL_KA + h)),
                  pl.BlockSpec((1, seq, LANES), lambda bi, h, i: (bi, 0, COL_VA + h)),
                  pl.BlockSpec((1, lc, LANES), lambda bi, h, i: (bi, 0, COL_KA + h)),
                  pl.BlockSpec((1, lc, LANES), lambda bi, h, i: (bi, 0, COL_VA + h)),
                  pl.BlockSpec((1, 3, NA_TQ, NA_TK), lambda bi, h, i: (h, 0, 0, 0))],
        out_specs=pl.BlockSpec((1, NA_TQ, LANES), lambda bi, h, i: (bi, i, h)),
        out_shape=jax.ShapeDtypeStruct((b, seq, NA_HEADS * HEAD_DIM), BF16),
        compiler_params=_cparams(("arbitrary", "arbitrary", "arbitrary")),
        name="na_attn",
    )(p, p, p, pc, pc, bias)


def _sw_kernel(sink_ref, q_ref, k_ref, v_ref, kc_ref, vc_ref, mask_ref, o_ref, *, nblk, seq):
    hkv = pl.program_id(1)
    blk = pl.program_id(2)
    start = pl.multiple_of(jnp.clip(blk * SW_TQ - SW_WINDOW, 0, seq - SW_TK), SW_WINDOW)
    var = jnp.where(blk == 0, 0, jnp.where(blk == nblk - 1, 2, 1))
    scale = HEAD_DIM ** -0.5
    kw = k_ref[0, pl.ds(start, SW_TK), :]
    vw = v_ref[0, pl.ds(start, SW_TK), :]
    kc = kc_ref[0]
    vc = vc_ref[0]
    mask = mask_ref[var]
    for g in range(SW_GROUP):
        q = q_ref[0, :, g * LANES:(g + 1) * LANES]
        sink = sink_ref[hkv * SW_GROUP + g]
        s_w = _dot_nt(q, kw) * scale + mask
        s_c = _dot_nt(q, kc) * scale
        m = jnp.maximum(s_w.max(axis=-1, keepdims=True), s_c.max(axis=-1, keepdims=True))
        m = jnp.maximum(m, sink)
        p_w = jnp.exp(s_w - m)
        p_c = jnp.exp(s_c - m)
        l = p_w.sum(axis=-1, keepdims=True) + p_c.sum(axis=-1, keepdims=True) + jnp.exp(sink - m)
        o = _dot(p_w.astype(BF16), vw) + _dot(p_c.astype(BF16), vc)
        o_ref[0, :, g * LANES:(g + 1) * LANES] = (o / l).astype(BF16)


def _sw_call(p, pc, sink, mask):
    b, seq, _ = p.shape
    lc = pc.shape[1]
    nblk = seq // SW_TQ
    gw = SW_GROUP * LANES
    grid_spec = pltpu.PrefetchScalarGridSpec(
        num_scalar_prefetch=1,
        grid=(b, SW_KV_HEADS, nblk),
        in_specs=[pl.BlockSpec((1, SW_TQ, gw), lambda bi, h, i, s: (bi, i, COL_QB // SW_GROUP + h)),
                  pl.BlockSpec((1, seq, LANES), lambda bi, h, i, s: (bi, 0, COL_KB + h)),
                  pl.BlockSpec((1, seq, LANES), lambda bi, h, i, s: (bi, 0, COL_VB + h)),
                  pl.BlockSpec((1, lc, LANES), lambda bi, h, i, s: (bi, 0, COL_KB + h)),
                  pl.BlockSpec((1, lc, LANES), lambda bi, h, i, s: (bi, 0, COL_VB + h)),
                  pl.BlockSpec((3, SW_TQ, SW_TK), lambda bi, h, i, s: (0, 0, 0))],
        out_specs=pl.BlockSpec((1, SW_TQ, gw), lambda bi, h, i, s: (bi, i, h)))
    return pl.pallas_call(
        functools.partial(_sw_kernel, nblk=nblk, seq=seq),
        grid_spec=grid_spec,
        out_shape=jax.ShapeDtypeStruct((b, seq, SW_HEADS * HEAD_DIM), BF16),
        compiler_params=_cparams(("arbitrary", "arbitrary", "arbitrary")),
        name="sw_attn",
    )(sink, p, p, p, pc, pc, mask)


def _df_lambda(lam_ref, lam_init):
    lp = lam_ref[0]
    s1 = jnp.sum(jnp.sum(lp[0:1] * lp[1:2], axis=-1, keepdims=True), axis=0, keepdims=True)
    s2 = jnp.sum(jnp.sum(lp[2:3] * lp[3:4], axis=-1, keepdims=True), axis=0, keepdims=True)
    return jnp.exp(s1) - jnp.exp(s2) + lam_init


def _df_kernel(lam_ref, g_ref, q_ref, *refs, lam_init, n_latent_blocks, tq):
    if n_latent_blocks:
        k_ref, v_ref, kc_ref, vc_ref, o_ref, m_scr, l_scr, acc_scr = refs
    else:
        kc_ref, vc_ref, o_ref, m_scr, l_scr, acc_scr = refs
    q = q_ref[0]
    lane = lax.broadcasted_iota(jnp.int32, q.shape, 1)
    zero = jnp.zeros_like(q)
    q2 = jnp.concatenate([jnp.where(lane < DF_QK_DIM, q, zero),
                          jnp.where(lane < DF_QK_DIM, zero, q)], axis=0)
    m_scr[...] = jnp.full(m_scr.shape, NEG, F32)
    l_scr[...] = jnp.zeros(l_scr.shape, F32)
    acc_scr[...] = jnp.zeros(acc_scr.shape, F32)

    def step(kb, vb):
        s = _dot_nt(q2, kb)
        m_prev = m_scr[...]
        m_new = jnp.maximum(m_prev, s.max(axis=-1, keepdims=True))
        alpha = jnp.exp(m_prev - m_new)
        pr = jnp.exp(s - m_new)
        l_scr[...] = alpha * l_scr[...] + pr.sum(axis=-1, keepdims=True)
        acc_scr[...] = alpha * acc_scr[...] + _dot(pr.astype(BF16), vb)
        m_scr[...] = m_new

    if n_latent_blocks:
        def body(i, carry):
            off = pl.multiple_of(i * DF_TK, DF_TK)
            step(k_ref[0, pl.ds(off, DF_TK), :], v_ref[0, pl.ds(off, DF_TK), :])
            return carry
        lax.fori_loop(0, n_latent_blocks, body, 0)
    step(kc_ref[0], vc_ref[0])

    lam = _df_lambda(lam_ref, lam_init)
    o2 = acc_scr[...] / l_scr[...]
    o = o2[:tq] - lam * o2[tq:]
    o_ref[0] = (_rms(o, g_ref[...]) * (1.0 - lam_init)).astype(BF16)


def _df_call(pq, pk, pc, lam_p, subln_g, lam_init, with_latent):
    b, lq, _ = pq.shape
    lc = pc.shape[1]
    tq = min(DF_TQ, lq)
    in_specs = [pl.BlockSpec((1, 4, DF_QK_DIM), lambda bi, h, i: (0, 0, 0)),
                pl.BlockSpec((1, DF_V_DIM), lambda bi, h, i: (0, 0)),
                pl.BlockSpec((1, tq, LANES), lambda bi, h, i: (bi, i, COL_QC + h))]
    args = [lam_p, subln_g, pq]
    nlb = 0
    if with_latent:
        seq = pk.shape[1]
        nlb = seq // DF_TK
        in_specs += [pl.BlockSpec((1, seq, LANES), lambda bi, h, i: (bi, 0, COL_KC + h)),
                     pl.BlockSpec((1, seq, LANES), lambda bi, h, i: (bi, 0, COL_VC + h))]
        args += [pk, pk]
    in_specs += [pl.BlockSpec((1, lc, LANES), lambda bi, h, i: (bi, 0, COL_KC + h)),
                 pl.BlockSpec((1, lc, LANES), lambda bi, h, i: (bi, 0, COL_VC + h))]
    args += [pc, pc]
    return pl.pallas_call(
        functools.partial(_df_kernel, lam_init=lam_init, n_latent_blocks=nlb, tq=tq),
        grid=(b, DF_HEADS, lq // tq),
        in_specs=in_specs,
        out_specs=pl.BlockSpec((1, tq, LANES), lambda bi, h, i: (bi, i, h)),
        out_shape=jax.ShapeDtypeStruct((b, lq, DF_HEADS * DF_V_DIM), BF16),
        scratch_shapes=[pltpu.VMEM((2 * tq, 1), F32), pltpu.VMEM((2 * tq, 1), F32),
                        pltpu.VMEM((2 * tq, DF_V_DIM), F32)],
        compiler_params=_cparams(("arbitrary", "arbitrary", "arbitrary")),
        name="df_attn" if with_latent else "df_attn_ctx",
    )(*args)


def _ctx_kernel(sink_ref, q_ref, k_ref, v_ref, o_ref, *, group, has_sink):
    hkv = pl.program_id(1)
    scale = HEAD_DIM ** -0.5
    k = k_ref[0]
    v = v_ref[0]
    for g in range(group):
        q = q_ref[0, :, g * LANES:(g + 1) * LANES]
        s = _dot_nt(q, k) * scale
        m = s.max(axis=-1, keepdims=True)
        if has_sink:
            sink = sink_ref[hkv * group + g]
            m = jnp.maximum(m, sink)
        pr = jnp.exp(s - m)
        l = pr.sum(axis=-1, keepdims=True)
        if has_sink:
            l = l + jnp.exp(sink - m)
        o_ref[0, :, g * LANES:(g + 1) * LANES] = (_dot(pr.astype(BF16), v) / l).astype(BF16)


def _ctx_call(pc, sink, col_q, col_k, col_v, n_kv, group, has_sink):
    b, lc, _ = pc.shape
    gw = group * LANES
    grid_spec = pltpu.PrefetchScalarGridSpec(
        num_scalar_prefetch=1,
        grid=(b, n_kv),
        in_specs=[pl.BlockSpec((1, lc, gw), lambda bi, h, s: (bi, 0, col_q // group + h)),
                  pl.BlockSpec((1, lc, LANES), lambda bi, h, s: (bi, 0, col_k + h)),
                  pl.BlockSpec((1, lc, LANES), lambda bi, h, s: (bi, 0, col_v + h))],
        out_specs=pl.BlockSpec((1, lc, gw), lambda bi, h, s: (bi, 0, h)))
    return pl.pallas_call(
        functools.partial(_ctx_kernel, group=group, has_sink=has_sink),
        grid_spec=grid_spec,
        out_shape=jax.ShapeDtypeStruct((b, lc, n_kv * gw), BF16),
        compiler_params=_cparams(("arbitrary", "arbitrary")),
        name="ctx_attn",
    )(sink, pc, pc, pc)


def _outproj_kernel(oa_ref, ob_ref, oc_ref, wa_ref, wb_ref, wc_ref, x_ref, gate_ref, g_ref,
                    shift_ref, scale_ref, *refs, with_router):
    if with_router:
        rh_ref, rl_ref, xo_ref, ho_ref, go_ref = refs
    else:
        xo_ref, ho_ref = refs
    y = _dot(oa_ref[...], wa_ref[...]) + _dot(ob_ref[...], wb_ref[...]) + _dot(oc_ref[...], wc_ref[...])
    xn = x_ref[...] + gate_ref[0] * y
    xo_ref[...] = xn
    h = _rms(xn, g_ref[...]) * (1.0 + scale_ref[0]) + shift_ref[0]
    hb = h.astype(BF16)
    ho_ref[...] = hb
    if with_router:
        hl = (h - hb.astype(F32)).astype(BF16)
        logits = _dot(hb, rh_ref[...]) + (_dot(hb, rl_ref[...]) + _dot(hl, rh_ref[...]))
        lane = lax.broadcasted_iota(jnp.int32, logits.shape, 1).astype(F32)
        logits = jnp.where(lane < N_EXPERTS, logits, NEG)
        v1 = logits.max(axis=-1, keepdims=True)
        i1 = jnp.where(logits == v1, lane, float(LANES)).min(axis=-1, keepdims=True)
        rest = jnp.where(lane == i1, NEG, logits)
        v2 = rest.max(axis=-1, keepdims=True)
        i2 = jnp.where(rest == v2, lane, float(LANES)).min(axis=-1, keepdims=True)
        e2 = jnp.exp(v2 - v1)
        den = 1.0 + e2
        go_ref[...] = jnp.where(lane == i1, 1.0 / den, 0.0) + jnp.where(lane == i2, e2 / den, 0.0)


def _outproj_call(oa, ob, oc, w_out, x2d, gate, g2, shift, scale, seq, tm, router=None):
    t_rows, d = x2d.shape
    tpb = seq // tm
    wa, wb, wc = oa.shape[1], ob.shape[1], oc.shape[1]
    assert wa == wb and (wa + wb) % wc == 0
    row = lambda i: (i, 0)
    fixed = lambda i: (0, 0)
    per_b = lambda i: (i // tpb, 0, 0)
    in_specs = [pl.BlockSpec((tm, wa), row), pl.BlockSpec((tm, wb), row), pl.BlockSpec((tm, wc), row),
                pl.BlockSpec((wa, d), lambda i: (0, 0)), pl.BlockSpec((wb, d), lambda i: (1, 0)),
                pl.BlockSpec((wc, d), lambda i: ((wa + wb) // wc, 0)),
                pl.BlockSpec((tm, d), row), pl.BlockSpec((1, 1, d), per_b), pl.BlockSpec((1, d), fixed),
                pl.BlockSpec((1, 1, d), per_b), pl.BlockSpec((1, 1, d), per_b)]
    args = [oa, ob, oc, w_out, w_out, w_out, x2d, gate, g2, shift, scale]
    out_specs = [pl.BlockSpec((tm, d), row), pl.BlockSpec((tm, d), row)]
    out_shape = [jax.ShapeDtypeStruct((t_rows, d), F32), jax.ShapeDtypeStruct((t_rows, d), BF16)]
    if router is not None:
        in_specs += [pl.BlockSpec((d, LANES), fixed), pl.BlockSpec((d, LANES), fixed)]
        args += list(router)
        out_specs.append(pl.BlockSpec((tm, LANES), row))
        out_shape.append(jax.ShapeDtypeStruct((t_rows, LANES), F32))
    return pl.pallas_call(
        functools.partial(_outproj_kernel, with_router=router is not None),
        grid=(t_rows // tm,),
        in_specs=in_specs, out_specs=out_specs, out_shape=out_shape,
        compiler_params=_cparams(("arbitrary",)),
        name="outproj",
    )(*args)


def _ffn_kernel(h_ref, w1_ref, w3_ref, w2_ref, x_ref, gate_ref, *refs, n_exp):
    if n_exp > 1:
        rg_ref, o_ref, acc_scr, tot_scr = refs
    else:
        o_ref, acc_scr = refs
    e = pl.program_id(1)
    f = pl.program_id(2)
    nf = pl.num_programs(2)

    @pl.when(f == 0)
    def _():
        acc_scr[...] = jnp.zeros(acc_scr.shape, F32)

    h = h_ref[...]
    a = _dot(h, w1_ref[0])
    b = _dot(h, w3_ref[0])
    act = (a * jax.nn.sigmoid(a) * b).astype(BF16)
    acc_scr[...] += _dot(act, w2_ref[0])

    if n_exp > 1:
        @pl.when(f == nf - 1)
        def _():
            rg = rg_ref[...]
            lane = lax.broadcasted_iota(jnp.int32, rg.shape, 1)
            row_gate = jnp.sum(jnp.where(lane == e, rg, 0.0), axis=-1, keepdims=True)
            part = row_gate * acc_scr[...]

            @pl.when(e == 0)
            def _():
                tot_scr[...] = part

            @pl.when(e > 0)
            def _():
                tot_scr[...] += part

        @pl.when((f == nf - 1) & (e == n_exp - 1))
        def _():
            o_ref[...] = x_ref[...] + gate_ref[0] * tot_scr[...]
    else:
        @pl.when(f == nf - 1)
        def _():
            o_ref[...] = x_ref[...] + gate_ref[0] * acc_scr[...]


def _ffn_call(h, w1, w3, w2, x2d, gate, seq, tm, tf=512, row_gates=None):
    t_rows, d = x2d.shape
    n_exp, _, ff = w1.shape
    tpb = seq // tm
    in_specs = [pl.BlockSpec((tm, d), lambda i, e, f: (i, 0)),
                pl.BlockSpec((1, d, tf), lambda i, e, f: (e, 0, f)),
                pl.BlockSpec((1, d, tf), lambda i, e, f: (e, 0, f)),
                pl.BlockSpec((1, tf, d), lambda i, e, f: (e, f, 0)),
                pl.BlockSpec((tm, d), lambda i, e, f: (i, 0)),
                pl.BlockSpec((1, 1, d), lambda i, e, f: (i // tpb, 0, 0))]
    args = [h, w1, w3, w2, x2d, gate]
    scratch = [pltpu.VMEM((tm, d), F32)]
    if n_exp > 1:
        in_specs.append(pl.BlockSpec((tm, LANES), lambda i, e, f: (i, 0)))
        args.append(row_gates)
        scratch.append(pltpu.VMEM((tm, d), F32))
    return pl.pallas_call(
        functools.partial(_ffn_kernel, n_exp=n_exp),
        grid=(t_rows // tm, n_exp, ff // tf),
        in_specs=in_specs,
        out_specs=pl.BlockSpec((tm, d), lambda i, e, f: (i, 0)),
        out_shape=jax.ShapeDtypeStruct((t_rows, d), F32),
        scratch_shapes=scratch,
        compiler_params=_cparams(("arbitrary", "arbitrary", "arbitrary")),
        name="ffn" if n_exp == 1 else "moe_ffn",
    )(*args)


def _rope_table(n_tokens, head_dim):
    axis_dim = head_dim // 2
    inv_freq = ROPE_THETA ** (-jnp.arange(0, axis_dim, 2, dtype=F32) / axis_dim)
    t = jnp.arange(n_tokens, dtype=jnp.int32)
    row = (t // GRID_W).astype(F32)
    col = (t % GRID_W).astype(F32)
    ang_r = row[:, None] * inv_freq[None, :]
    ang_c = col[:, None] * inv_freq[None, :]
    cos = jnp.concatenate([jnp.cos(ang_r)] * 2 + [jnp.cos(ang_c)] * 2, axis=-1)
    sin = jnp.concatenate([-jnp.sin(ang_r), jnp.sin(ang_r), -jnp.sin(ang_c), jnp.sin(ang_c)], axis=-1)
    reps = LANES // head_dim
    return jnp.tile(cos, (1, reps)), jnp.tile(sin, (1, reps))


def _identity_rope_table(n_tokens):
    return jnp.ones((n_tokens, LANES), F32), jnp.zeros((n_tokens, LANES), F32)


def _na_bias_index(seq):
    rows = seq // GRID_W
    nblk = rows // NA_QROWS
    qi = np.arange(NA_TQ)
    kj = np.arange(NA_TK)
    valid, ir, ic = [], [], []
    for blk in (0, 1, nblk - 1):
        r_base = blk * NA_QROWS
        u0 = int(np.clip(r_base - NA_WIN_R // 2, 0, rows - NA_KROWS))
        r = (r_base + qi // GRID_W)[:, None]
        c = (qi % GRID_W)[:, None]
        kr = (u0 + kj // GRID_W)[None, :]
        kc = (kj % GRID_W)[None, :]
        r0 = np.clip(r - NA_WIN_R // 2, 0, rows - NA_WIN_R)
        c0 = np.clip(c - NA_WIN_C // 2, 0, GRID_W - NA_WIN_C)
        ok = (kr >= r0) & (kr < r0 + NA_WIN_R) & (kc >= c0) & (kc < c0 + NA_WIN_C)
        valid.append(ok)
        ir.append(np.clip(kr - r + NA_WIN_R - 1, 0, 2 * NA_WIN_R - 2) + 0 * kc)
        ic.append(np.clip(kc - c + NA_WIN_C - 1, 0, 2 * NA_WIN_C - 2) + 0 * kr)
    return np.stack(valid), np.stack(ir), np.stack(ic)


def _na_bias_table(rpb, seq):
    valid, ir, ic = _na_bias_index(seq)
    vals = rpb.astype(F32)[:, ir, ic]
    return jnp.where(valid[None], vals, NEG)


def _sw_mask_table(seq):
    nblk = seq // SW_TQ
    out = []
    for blk in (0, 1, nblk - 1):
        start = int(np.clip(blk * SW_TQ - SW_WINDOW, 0, seq - SW_TK))
        qpos = (blk * SW_TQ + np.arange(SW_TQ))[:, None]
        kpos = (start + np.arange(SW_TK))[None, :]
        out.append(np.where(np.abs(qpos - kpos) <= SW_WINDOW, 0.0, NEG))
    return jnp.asarray(np.stack(out), F32)


def _col_gains(na_q, na_k, sw_q, sw_k, df_q, df_k):
    one = lambda n: jnp.ones((n,), F32)
    df_scale = DF_QK_DIM ** -0.5
    return jnp.concatenate([
        jnp.tile(na_q, NA_HEADS), jnp.tile(na_k, NA_HEADS), one(NA_HEADS * HEAD_DIM),
        jnp.tile(sw_q, SW_HEADS), jnp.tile(sw_k, SW_KV_HEADS), one(SW_KV_HEADS * HEAD_DIM),
        jnp.tile(df_q * df_scale, 2 * DF_HEADS), jnp.tile(df_k, 2 * DF_HEADS), one(DF_HEADS * DF_V_DIM),
    ]).astype(F32)[None, :]


def kernel(x, c, ctx, c_ctx, ada_w, ada_b, norm1_g, norm2_g, w_in, w_out, na_q_norm, na_k_norm, na_rpb,
           sw_q_norm, sw_k_norm, sw_sink, df_q_norm, df_k_norm, df_lambda, df_subln_g,
           ffn_w1, ffn_w3, ffn_w2, moe_router, moe_w1, moe_w3, moe_w2):
    b, seq, d = x.shape
    lc = ctx.shape[1]
    depth = ada_w.shape[0]
    assert d == D_MODEL and seq % (GRID_W * NA_QROWS) == 0 and seq % DF_TK == 0

    c_rows = jnp.concatenate([c, c_ctx[None], jnp.zeros((16 - b - 1, d), F32)], axis=0)
    mod = _ada_call(c_rows, ada_w, ada_b)

    tabs_x = _rope_table(seq, HEAD_DIM) + _rope_table(seq, DF_QK_DIM)
    tabs_c = _identity_rope_table(lc) * 2
    sw_mask = _sw_mask_table(seq)

    x2 = x.reshape(b * seq, d)
    xc2 = ctx.reshape(b * lc, d)
    tm_x = 512
    tm_c = lc

    for i in range(depth):
        last = i == depth - 1
        m = mod[i].reshape(16, 6, d)
        mx = [m[:b, j][:, None, :] for j in range(6)]
        mc = [jnp.broadcast_to(m[b:b + 1, j][:, None, :], (b, 1, d)) for j in range(6)]
        lam_init = 0.8 - 0.6 * math.exp(-0.3 * i)
        gains = _col_gains(na_q_norm[i], na_k_norm[i], sw_q_norm[i], sw_k_norm[i], df_q_norm[i], df_k_norm[i])
        g1 = norm1_g[i][None, :]
        g2 = norm2_g[i][None, :]
        w_in_b = w_in[i].astype(BF16)
        w_out_b = w_out[i].astype(BF16)
        sink = sw_sink[i].astype(F32)
        lam_p = df_lambda[i][None].astype(F32)
        subln = df_subln_g[i][None, :]

        p = _proj_call(x2, mx[0], mx[1], g1, w_in_b, gains, tabs_x, seq, 1024).reshape(b, seq, IN_WIDTH)
        pc = _proj_call(xc2, mc[0], mc[1], g1, w_in_b, gains, tabs_c, lc, tm_c).reshape(b, lc, IN_WIDTH)
        oa = _na_call(p, pc, _na_bias_table(na_rpb[i], seq))
        ob = _sw_call(p, pc, sink, sw_mask)
        oc = _df_call(p, p, pc, lam_p, subln, lam_init, True)

        j = i // 2
        moe = i % 2 == 1
        if moe:
            r = jnp.pad(moe_router[j].astype(F32), ((0, 0), (0, LANES - N_EXPERTS)))
            r_hi = r.astype(BF16)
            r_lo = (r - r_hi.astype(F32)).astype(BF16)
            router = (r_hi, r_lo)
            w1, w3, w2 = moe_w1[j].astype(BF16), moe_w3[j].astype(BF16), moe_w2[j].astype(BF16)
        else:
            router = None
            w1, w3, w2 = (ffn_w1[j][None].astype(BF16), ffn_w3[j][None].astype(BF16),
                          ffn_w2[j][None].astype(BF16))

        def mix_and_ffn(oa_, ob_, oc_, xs, mv, seq_, tm_):
            flat = lambda o: o.reshape(xs.shape[0], o.shape[-1])
            res = _outproj_call(flat(oa_), flat(ob_), flat(oc_), w_out_b, xs, mv[2], g2, mv[3], mv[4],
                                seq_, min(tm_, 256), router)
            gates = res[2] if moe else None
            return _ffn_call(res[1], w1, w3, w2, res[0], mv[5], seq_, tm_, row_gates=gates)

        if not last:
            coa = _ctx_call(pc, sink, COL_QA, COL_KA, COL_VA, NA_HEADS, 1, False)
            cob = _ctx_call(pc, sink, COL_QB, COL_KB, COL_VB, SW_KV_HEADS, SW_GROUP, True)
            coc = _df_call(pc, None, pc, lam_p, subln, lam_init, False)
            xc2 = mix_and_ffn(coa, cob, coc, xc2, mc, lc, tm_c)
        x2 = mix_and_ffn(oa, ob, oc, x2, mx, seq, tm_x)
    return x2.reshape(b, seq, d)
```

```python
import functools
import math

import numpy as np
import jax
import jax.numpy as jnp
from jax import lax
from jax.experimental import pallas as pl
from jax.experimental.pallas import tpu as pltpu

F32 = jnp.float32
BF16 = jnp.bfloat16

D_MODEL = 2048
GRID_W = 64
HEAD_DIM = 128
NA_HEADS = 6
NA_WIN_R = 8
NA_WIN_C = 16
SW_HEADS = 6
SW_KV_HEADS = 2
SW_GROUP = SW_HEADS // SW_KV_HEADS
SW_WINDOW = 128
DF_HEADS = 4
DF_QK_DIM = 64
DF_V_DIM = 128
IN_WIDTH = 5120
D_FF = 5632
N_EXPERTS = 8
ROPE_THETA = 10000.0
NORM_EPS = 1e-6
NEG = -1e30

LANES = 128
VMEM_LIMIT = 56 * 1024 * 1024

COL_QA, COL_KA, COL_VA = 0, 6, 12
COL_QB, COL_KB, COL_VB = 18, 24, 26
COL_QC, COL_KC, COL_VC = 28, 32, 36

T_PLAIN, T_NORM, T_NORM_ROPE_B, T_NORM_ROPE_C = 0, 1, 2, 3
CHUNK_TYPES = np.array([T_NORM] * 6 + [T_PLAIN] * 3 + [T_NORM_ROPE_B] * 4 + [T_PLAIN]
                       + [T_NORM_ROPE_C] * 4 + [T_PLAIN] * 2, dtype=np.int32)

NA_QROWS = 4
NA_KROWS = 12
NA_TQ = NA_QROWS * GRID_W
NA_TK = NA_KROWS * GRID_W
SW_TQ = 256
SW_TK = SW_TQ + 2 * SW_WINDOW
DF_TQ = 256
DF_TK = 512


def _cparams(sem):
    return pltpu.CompilerParams(dimension_semantics=sem, vmem_limit_bytes=VMEM_LIMIT)


def _dot(a, b):
    return jnp.dot(a, b, preferred_element_type=F32)


def _dot_nt(a, b):
    return lax.dot_general(a, b, (((1,), (1,)), ((), ())), preferred_element_type=F32)


def _ada_kernel(c_ref, w_ref, b_ref, o_ref):
    c = c_ref[...]
    sc = (c * jax.nn.sigmoid(c)).astype(BF16)
    o_ref[0] = _dot(sc, w_ref[0].astype(BF16)) + b_ref[0]


def _ada_call(c_rows, ada_w, ada_b):
    depth, d, n = ada_w.shape
    rows = c_rows.shape[0]
    tn = 1024
    return pl.pallas_call(
        _ada_kernel,
        grid=(depth, n // tn),
        in_specs=[pl.BlockSpec((rows, d), lambda l, j: (0, 0)),
                  pl.BlockSpec((1, d, tn), lambda l, j: (l, 0, j)),
                  pl.BlockSpec((1, 1, tn), lambda l, j: (l, 0, j))],
        out_specs=pl.BlockSpec((1, rows, tn), lambda l, j: (l, 0, j)),
        out_shape=jax.ShapeDtypeStruct((depth, rows, n), F32),
        compiler_params=_cparams(("arbitrary", "arbitrary")),
        name="ada",
    )(c_rows, ada_w, ada_b.reshape(depth, 1, n))


def _rms(v, gain):
    ms = jnp.mean(v * v, axis=-1, keepdims=True)
    return v * lax.rsqrt(ms + NORM_EPS) * gain


def _rms_half(v, gain, lane):
    lo = lane < DF_QK_DIM
    sq = v * v
    s_lo = jnp.sum(jnp.where(lo, sq, 0.0), axis=-1, keepdims=True)
    s_hi = jnp.sum(jnp.where(lo, 0.0, sq), axis=-1, keepdims=True)
    ms = jnp.where(lo, s_lo, s_hi) * (1.0 / DF_QK_DIM)
    return v * lax.rsqrt(ms + NORM_EPS) * gain


def _rope(v, cos, sin, half, lane):
    fwd = pltpu.roll(v, LANES - half, axis=1)
    bwd = pltpu.roll(v, half, axis=1)
    partner = jnp.where((lane & half) == 0, fwd, bwd)
    return v * cos + partner * sin


def _proj_kernel(types_ref, x_ref, shift_ref, scale_ref, g_ref, w_ref, gain_ref,
                 cb_ref, sb_ref, cc_ref, sc_ref, o_ref, h_scr, *, nch):
    j = pl.program_id(1)

    @pl.when(j == 0)
    def _():
        x = x_ref[...]
        h = _rms(x, g_ref[...]) * (1.0 + scale_ref[0]) + shift_ref[0]
        h_scr[...] = h.astype(BF16)

    p = _dot(h_scr[...], w_ref[...])
    tm = p.shape[0]
    lane = lax.broadcasted_iota(jnp.int32, (tm, LANES), 1)
    for s in range(nch):
        t = types_ref[j * nch + s]
        for hh in range(2):
            c0 = s * 256 + hh * LANES
            v = p[:, c0:c0 + LANES]
            gain = gain_ref[:, c0:c0 + LANES]

            @pl.when(t == T_PLAIN)
            def _():
                o_ref[:, c0:c0 + LANES] = v.astype(BF16)

            @pl.when(t == T_NORM)
            def _():
                o_ref[:, c0:c0 + LANES] = _rms(v, gain).astype(BF16)

            @pl.when(t == T_NORM_ROPE_B)
            def _():
                y = _rope(_rms(v, gain), cb_ref[...], sb_ref[...], HEAD_DIM // 4, lane)
                o_ref[:, c0:c0 + LANES] = y.astype(BF16)

            @pl.when(t == T_NORM_ROPE_C)
            def _():
                y = _rope(_rms_half(v, gain, lane), cc_ref[...], sc_ref[...], DF_QK_DIM // 4, lane)
                o_ref[:, c0:c0 + LANES] = y.astype(BF16)


def _proj_call(x2d, shift, scale, g, w, gains, tabs, seq, tm, tn=512):
    t_rows, d = x2d.shape
    n = w.shape[1]
    tpb = seq // tm
    nch = tn // 256
    grid_spec = pltpu.PrefetchScalarGridSpec(
        num_scalar_prefetch=1,
        grid=(t_rows // tm, n // tn),
        in_specs=[pl.BlockSpec((tm, d), lambda i, j, t: (i, 0)),
                  pl.BlockSpec((1, 1, d), lambda i, j, t: (i // tpb, 0, 0)),
                  pl.BlockSpec((1, 1, d), lambda i, j, t: (i // tpb, 0, 0)),
                  pl.BlockSpec((1, d), lambda i, j, t: (0, 0)),
                  pl.BlockSpec((d, tn), lambda i, j, t: (0, j)),
                  pl.BlockSpec((1, tn), lambda i, j, t: (0, j))]
                 + [pl.BlockSpec((tm, LANES), lambda i, j, t: (i % tpb, 0))] * 4,
        out_specs=pl.BlockSpec((tm, tn), lambda i, j, t: (i, j)),
        scratch_shapes=[pltpu.VMEM((tm, d), BF16)])
    return pl.pallas_call(
        functools.partial(_proj_kernel, nch=nch),
        grid_spec=grid_spec,
        out_shape=jax.ShapeDtypeStruct((t_rows, n), BF16),
        compiler_params=_cparams(("arbitrary", "arbitrary")),
        name="proj",
    )(jnp.asarray(CHUNK_TYPES), x2d, shift, scale, g, w, gains, *tabs)


def _softmax_parts(parts):
    m = parts[0].max(axis=-1, keepdims=True)
    for s in parts[1:]:
        m = jnp.maximum(m, s.max(axis=-1, keepdims=True))
    return m


def _na_kernel(q_ref, k_ref, v_ref, kc_ref, vc_ref, bias_ref, o_ref, *, nblk):
    blk = pl.program_id(2)
    u0 = jnp.clip(NA_QROWS * blk - NA_WIN_R // 2, 0, GRID_W - NA_KROWS)
    start = pl.multiple_of(u0 * GRID_W, GRID_W)
    var = jnp.where(blk == 0, 0, jnp.where(blk == nblk - 1, 2, 1))
    scale = HEAD_DIM ** -0.5
    q = q_ref[0]
    kw = k_ref[0, pl.ds(start, NA_TK), :]
    vw = v_ref[0, pl.ds(start, NA_TK), :]
    s_nb = _dot_nt(q, kw) * scale + bias_ref[0, var]
    s_cx = _dot_nt(q, kc_ref[0]) * scale
    m = jnp.maximum(s_nb.max(axis=-1, keepdims=True), s_cx.max(axis=-1, keepdims=True))
    p_nb = jnp.exp(s_nb - m)
    p_cx = jnp.exp(s_cx - m)
    l = p_nb.sum(axis=-1, keepdims=True) + p_cx.sum(axis=-1, keepdims=True)
    o = _dot(p_nb.astype(BF16), vw) + _dot(p_cx.astype(BF16), vc_ref[0])
    o_ref[0] = (o / l).astype(BF16)


def _na_call(p, pc, bias):
    b, seq, _ = p.shape
    lc = pc.shape[1]
    nblk = seq // NA_TQ
    return pl.pallas_call(
        functools.partial(_na_kernel, nblk=nblk),
        grid=(b, NA_HEADS, nblk),
        in_specs=[pl.BlockSpec((1, NA_TQ, LANES), lambda bi, h, i: (bi, i, COL_QA + h)),
                  pl.BlockSpec((1, seq, LANES), lambda bi, h, i: (bi, 0, COL_KA + h)),
                  pl.BlockSpec((1, seq, LANES), lambda bi, h, i: (bi, 0, COL_VA + h)),
                  pl.BlockSpec((1, lc, LANES), lambda bi, h, i: (bi, 0, COL_KA + h)),
                  pl.BlockSpec((1, lc, LANES), lambda bi, h, i: (bi, 0, COL_VA + h)),
                  pl.BlockSpec((1, 3, NA_TQ, NA_TK), lambda bi, h, i: (h, 0, 0, 0))],
        out_specs=pl.BlockSpec((1, NA_TQ, LANES), lambda bi, h, i: (bi, i, h)),
        out_shape=jax.ShapeDtypeStruct((b, seq, NA_HEADS * HEAD_DIM), BF16),
        compiler_params=_cparams(("arbitrary", "arbitrary", "arbitrary")),
        name="na_attn",
    )(p, p, p, pc, pc, bias)


def _sw_kernel(sink_ref, q_ref, k_ref, v_ref, kc_ref, vc_ref, mask_ref, o_ref, *, nblk, seq):
    hkv = pl.program_id(1)
    blk = pl.program_id(2)
    start = pl.multiple_of(jnp.clip(blk * SW_TQ - SW_WINDOW, 0, seq - SW_TK), SW_WINDOW)
    var = jnp.where(blk == 0, 0, jnp.where(blk == nblk - 1, 2, 1))
    scale = HEAD_DIM ** -0.5
    kw = k_ref[0, pl.ds(start, SW_TK), :]
    vw = v_ref[0, pl.ds(start, SW_TK), :]
    kc = kc_ref[0]
    vc = vc_ref[0]
    mask = mask_ref[var]
    for g in range(SW_GROUP):
        q = q_ref[0, :, g * LANES:(g + 1) * LANES]
        sink = sink_ref[hkv * SW_GROUP + g]
        s_w = _dot_nt(q, kw) * scale + mask
        s_c = _dot_nt(q, kc) * scale
        m = jnp.maximum(s_w.max(axis=-1, keepdims=True), s_c.max(axis=-1, keepdims=True))
        m = jnp.maximum(m, sink)
        p_w = jnp.exp(s_w - m)
        p_c = jnp.exp(s_c - m)
        l = p_w.sum(axis=-1, keepdims=True) + p_c.sum(axis=-1, keepdims=True) + jnp.exp(sink - m)
        o = _dot(p_w.astype(BF16), vw) + _dot(p_c.astype(BF16), vc)
        o_ref[0, :, g * LANES:(g + 1) * LANES] = (o / l).astype(BF16)


def _sw_call(p, pc, sink, mask):
    b, seq, _ = p.shape
    lc = pc.shape[1]
    nblk = seq // SW_TQ
    gw = SW_GROUP * LANES
    grid_spec = pltpu.PrefetchScalarGridSpec(
        num_scalar_prefetch=1,
        grid=(b, SW_KV_HEADS, nblk),
        in_specs=[pl.BlockSpec((1, SW_TQ, gw), lambda bi, h, i, s: (bi, i, COL_QB // SW_GROUP + h)),
                  pl.BlockSpec((1, seq, LANES), lambda bi, h, i, s: (bi, 0, COL_KB + h)),
                  pl.BlockSpec((1, seq, LANES), lambda bi, h, i, s: (bi, 0, COL_VB + h)),
                  pl.BlockSpec((1, lc, LANES), lambda bi, h, i, s: (bi, 0, COL_KB + h)),
                  pl.BlockSpec((1, lc, LANES), lambda bi, h, i, s: (bi, 0, COL_VB + h)),
                  pl.BlockSpec((3, SW_TQ, SW_TK), lambda bi, h, i, s: (0, 0, 0))],
        out_specs=pl.BlockSpec((1, SW_TQ, gw), lambda bi, h, i, s: (bi, i, h)))
    return pl.pallas_call(
        functools.partial(_sw_kernel, nblk=nblk, seq=seq),
        grid_spec=grid_spec,
        out_shape=jax.ShapeDtypeStruct((b, seq, SW_HEADS * HEAD_DIM), BF16),
        compiler_params=_cparams(("arbitrary", "arbitrary", "arbitrary")),
        name="sw_attn",
    )(sink, p, p, p, pc, pc, mask)


def _df_lambda(lam_ref, lam_init):
    lp = lam_ref[0]
    s1 = jnp.sum(jnp.sum(lp[0:1] * lp[1:2], axis=-1, keepdims=True), axis=0, keepdims=True)
    s2 = jnp.sum(jnp.sum(lp[2:3] * lp[3:4], axis=-1, keepdims=True), axis=0, keepdims=True)
    return jnp.exp(s1) - jnp.exp(s2) + lam_init


def _df_kernel(lam_ref, g_ref, q_ref, *refs, lam_init, with_latent, tq):
    if with_latent:
        k_ref, v_ref, kc_ref, vc_ref, o_ref = refs
    else:
        kc_ref, vc_ref, o_ref = refs
    q = q_ref[0]
    lane = lax.broadcasted_iota(jnp.int32, q.shape, 1)
    zero = jnp.zeros_like(q)
    q2 = jnp.concatenate([jnp.where(lane < DF_QK_DIM, q, zero),
                          jnp.where(lane < DF_QK_DIM, zero, q)], axis=0)
    chunks = []
    if with_latent:
        for c in range(k_ref.shape[1] // DF_TK):
            chunks.append((k_ref[0, c * DF_TK:(c + 1) * DF_TK, :], v_ref[0, c * DF_TK:(c + 1) * DF_TK, :]))
    chunks.append((kc_ref[0], vc_ref[0]))
    logits = [_dot_nt(q2, kb) for kb, _ in chunks]
    m = logits[0].max(axis=-1, keepdims=True)
    for s in logits[1:]:
        m = jnp.maximum(m, s.max(axis=-1, keepdims=True))
    l = jnp.zeros_like(m)
    acc = jnp.zeros((2 * tq, DF_V_DIM), F32)
    for s, (_, vb) in zip(logits, chunks):
        pr = jnp.exp(s - m)
        l = l + pr.sum(axis=-1, keepdims=True)
        acc = acc + _dot(pr.astype(BF16), vb)

    lam = _df_lambda(lam_ref, lam_init)
    o2 = acc / l
    o = o2[:tq] - lam * o2[tq:]
    o_ref[0] = (_rms(o, g_ref[...]) * (1.0 - lam_init)).astype(BF16)


def _df_call(pq, pk, pc, lam_p, subln_g, lam_init, with_latent):
    b, lq, _ = pq.shape
    lc = pc.shape[1]
    tq = min(DF_TQ, lq)
    in_specs = [pl.BlockSpec((1, 4, DF_QK_DIM), lambda bi, h, i: (0, 0, 0)),
                pl.BlockSpec((1, DF_V_DIM), lambda bi, h, i: (0, 0)),
                pl.BlockSpec((1, tq, LANES), lambda bi, h, i: (bi, i, COL_QC + h))]
    args = [lam_p, subln_g, pq]
    if with_latent:
        seq = pk.shape[1]
        in_specs += [pl.BlockSpec((1, seq, LANES), lambda bi, h, i: (bi, 0, COL_KC + h)),
                     pl.BlockSpec((1, seq, LANES), lambda bi, h, i: (bi, 0, COL_VC + h))]
        args += [pk, pk]
    in_specs += [pl.BlockSpec((1, lc, LANES), lambda bi, h, i: (bi, 0, COL_KC + h)),
                 pl.BlockSpec((1, lc, LANES), lambda bi, h, i: (bi, 0, COL_VC + h))]
    args += [pc, pc]
    return pl.pallas_call(
        functools.partial(_df_kernel, lam_init=lam_init, with_latent=with_latent, tq=tq),
        grid=(b, DF_HEADS, lq // tq),
        in_specs=in_specs,
        out_specs=pl.BlockSpec((1, tq, LANES), lambda bi, h, i: (bi, i, h)),
        out_shape=jax.ShapeDtypeStruct((b, lq, DF_HEADS * DF_V_DIM), BF16),
        compiler_params=_cparams(("arbitrary", "arbitrary", "arbitrary")),
        name="df_attn" if with_latent else "df_attn_ctx",
    )(*args)


def _ctx_kernel(sink_ref, q_ref, k_ref, v_ref, o_ref, *, group, has_sink):
    hkv = pl.program_id(1)
    scale = HEAD_DIM ** -0.5
    k = k_ref[0]
    v = v_ref[0]
    for g in range(group):
        q = q_ref[0, :, g * LANES:(g + 1) * LANES]
        s = _dot_nt(q, k) * scale
        m = s.max(axis=-1, keepdims=True)
        if has_sink:
            sink = sink_ref[hkv * group + g]
            m = jnp.maximum(m, sink)
        pr = jnp.exp(s - m)
        l = pr.sum(axis=-1, keepdims=True)
        if has_sink:
            l = l + jnp.exp(sink - m)
        o_ref[0, :, g * LANES:(g + 1) * LANES] = (_dot(pr.astype(BF16), v) / l).astype(BF16)


def _ctx_call(pc, sink, col_q, col_k, col_v, n_kv, group, has_sink):
    b, lc, _ = pc.shape
    gw = group * LANES
    grid_spec = pltpu.PrefetchScalarGridSpec(
        num_scalar_prefetch=1,
        grid=(b, n_kv),
        in_specs=[pl.BlockSpec((1, lc, gw), lambda bi, h, s: (bi, 0, col_q // group + h)),
                  pl.BlockSpec((1, lc, LANES), lambda bi, h, s: (bi, 0, col_k + h)),
                  pl.BlockSpec((1, lc, LANES), lambda bi, h, s: (bi, 0, col_v + h))],
        out_specs=pl.BlockSpec((1, lc, gw), lambda bi, h, s: (bi, 0, h)))
    return pl.pallas_call(
        functools.partial(_ctx_kernel, group=group, has_sink=has_sink),
        grid_spec=grid_spec,
        out_shape=jax.ShapeDtypeStruct((b, lc, n_kv * gw), BF16),
        compiler_params=_cparams(("arbitrary", "arbitrary")),
        name="ctx_attn",
    )(sink, pc, pc, pc)


def _outproj_kernel(oa_ref, ob_ref, oc_ref, wa_ref, wb_ref, wc_ref, x_ref, gate_ref, g_ref,
                    shift_ref, scale_ref, *refs, with_router):
    if with_router:
        rh_ref, rl_ref, xo_ref, ho_ref, go_ref = refs
    else:
        xo_ref, ho_ref = refs
    y = _dot(oa_ref[...], wa_ref[...]) + _dot(ob_ref[...], wb_ref[...]) + _dot(oc_ref[...], wc_ref[...])
    xn = x_ref[...] + gate_ref[0] * y
    xo_ref[...] = xn
    h = _rms(xn, g_ref[...]) * (1.0 + scale_ref[0]) + shift_ref[0]
    hb = h.astype(BF16)
    ho_ref[...] = h.astype(ho_ref.dtype)
    if with_router:
        hl = (h - hb.astype(F32)).astype(BF16)
        logits = _dot(hb, rh_ref[...]) + (_dot(hb, rl_ref[...]) + _dot(hl, rh_ref[...]))
        lane = lax.broadcasted_iota(jnp.int32, logits.shape, 1).astype(F32)
        logits = jnp.where(lane < N_EXPERTS, logits, NEG)
        v1 = logits.max(axis=-1, keepdims=True)
        i1 = jnp.where(logits == v1, lane, float(LANES)).min(axis=-1, keepdims=True)
        rest = jnp.where(lane == i1, NEG, logits)
        v2 = rest.max(axis=-1, keepdims=True)
        i2 = jnp.where(rest == v2, lane, float(LANES)).min(axis=-1, keepdims=True)
        e2 = jnp.exp(v2 - v1)
        den = 1.0 + e2
        go_ref[...] = (jnp.where(lane == 0.0, i1, 0.0) + jnp.where(lane == 1.0, i2, 0.0)
                       + jnp.where(lane == 2.0, 1.0 / den, 0.0) + jnp.where(lane == 3.0, e2 / den, 0.0))


def _outproj_call(oa, ob, oc, w_out, x2d, gate, g2, shift, scale, seq, tm, router=None):
    t_rows, d = x2d.shape
    tpb = seq // tm
    wa, wb, wc = oa.shape[1], ob.shape[1], oc.shape[1]
    assert wa == wb and (wa + wb) % wc == 0
    row = lambda i: (i, 0)
    fixed = lambda i: (0, 0)
    per_b = lambda i: (i // tpb, 0, 0)
    in_specs = [pl.BlockSpec((tm, wa), row), pl.BlockSpec((tm, wb), row), pl.BlockSpec((tm, wc), row),
                pl.BlockSpec((wa, d), lambda i: (0, 0)), pl.BlockSpec((wb, d), lambda i: (1, 0)),
                pl.BlockSpec((wc, d), lambda i: ((wa + wb) // wc, 0)),
                pl.BlockSpec((tm, d), row), pl.BlockSpec((1, 1, d), per_b), pl.BlockSpec((1, d), fixed),
                pl.BlockSpec((1, 1, d), per_b), pl.BlockSpec((1, 1, d), per_b)]
    args = [oa, ob, oc, w_out, w_out, w_out, x2d, gate, g2, shift, scale]
    out_specs = [pl.BlockSpec((tm, d), row), pl.BlockSpec((tm, d), row)]
    out_shape = [jax.ShapeDtypeStruct((t_rows, d), F32),
                 jax.ShapeDtypeStruct((t_rows, d), BF16 if router is None else F32)]
    if router is not None:
        in_specs += [pl.BlockSpec((d, LANES), fixed), pl.BlockSpec((d, LANES), fixed)]
        args += list(router)
        out_specs.append(pl.BlockSpec((tm, LANES), row))
        out_shape.append(jax.ShapeDtypeStruct((t_rows, LANES), F32))
    return pl.pallas_call(
        functools.partial(_outproj_kernel, with_router=router is not None),
        grid=(t_rows // tm,),
        in_specs=in_specs, out_specs=out_specs, out_shape=out_shape,
        compiler_params=_cparams(("arbitrary",)),
        name="outproj",
    )(*args)


def _swiglu_step(h, w1_ref, w3_ref, w2_ref, acc_scr):
    a = _dot(h, w1_ref[...])
    b = _dot(h, w3_ref[...])
    act = (a * jax.nn.sigmoid(a) * b).astype(BF16)
    acc_scr[...] += _dot(act, w2_ref[...])


def _ffn_kernel(h_ref, w1_ref, w3_ref, w2_ref, x_ref, gate_ref, o_ref, acc_scr):
    f = pl.program_id(1)

    @pl.when(f == 0)
    def _():
        acc_scr[...] = jnp.zeros(acc_scr.shape, F32)

    _swiglu_step(h_ref[...], w1_ref, w3_ref, w2_ref, acc_scr)

    @pl.when(f == pl.num_programs(1) - 1)
    def _():
        o_ref[...] = x_ref[...] + gate_ref[0] * acc_scr[...]


def _ffn_call(h, w1, w3, w2, x2d, gate, seq, tm, tf=512):
    t_rows, d = x2d.shape
    ff = w1.shape[1]
    tpb = seq // tm
    return pl.pallas_call(
        _ffn_kernel,
        grid=(t_rows // tm, ff // tf),
        in_specs=[pl.BlockSpec((tm, d), lambda i, f: (i, 0)),
                  pl.BlockSpec((d, tf), lambda i, f: (0, f)),
                  pl.BlockSpec((d, tf), lambda i, f: (0, f)),
                  pl.BlockSpec((tf, d), lambda i, f: (f, 0)),
                  pl.BlockSpec((tm, d), lambda i, f: (i, 0)),
                  pl.BlockSpec((1, 1, d), lambda i, f: (i // tpb, 0, 0))],
        out_specs=pl.BlockSpec((tm, d), lambda i, f: (i, 0)),
        out_shape=jax.ShapeDtypeStruct((t_rows, d), F32),
        scratch_shapes=[pltpu.VMEM((tm, d), F32)],
        compiler_params=_cparams(("arbitrary", "arbitrary")),
        name="ffn",
    )(h, w1, w3, w2, x2d, gate)


MOE_TM = 512
ROW_UNROLL = 8


def _route_plan(info, tm):
    t_rows = info.shape[0]
    e_flat = info[:, :2].astype(jnp.int32).T.reshape(-1)
    onehot = (e_flat[:, None] == jnp.arange(N_EXPERTS, dtype=jnp.int32)[None, :]).astype(jnp.int32)
    csum = jnp.cumsum(onehot, axis=0)
    counts = csum[-1]
    padded = ((counts + tm - 1) // tm) * tm
    pend = jnp.cumsum(padded)
    pstart = pend - padded
    dest = jnp.sum(onehot * (csum - 1 + pstart[None, :]), axis=1).astype(jnp.int32)
    n_tiles = (2 * t_rows) // tm + N_EXPERTS
    n_used = (pend[-1] // tm).astype(jnp.int32)
    tile_idx = jnp.arange(n_tiles, dtype=jnp.int32)
    owner = jnp.sum((tile_idx[:, None] * tm >= pend[None, :]).astype(jnp.int32), axis=1)
    last_owner = jnp.sum(((n_used - 1) * tm >= pend).astype(jnp.int32))
    tile_expert = jnp.where(tile_idx < n_used, owner, last_owner).astype(jnp.int32)
    return dest, tile_expert, n_used.reshape(1)


def _row_copy_loop(n_rows, start_one):
    def issue(r, carry):
        for cp in start_one(r):
            cp.start()
        return carry
    lax.fori_loop(0, n_rows, issue, 0, unroll=ROW_UNROLL)

    def drain(r, carry):
        for cp in start_one(r):
            cp.wait()
        return carry
    lax.fori_loop(0, n_rows, drain, 0, unroll=ROW_UNROLL)


def _scatter_kernel(dest_ref, h_hbm, xg_in, xg_out, sem, *, ts, t_rows):
    del xg_in
    base = pl.program_id(0) * ts

    def copies(r):
        t = base + r
        return [pltpu.make_async_copy(h_hbm.at[pl.ds(t, 1)], xg_out.at[pl.ds(dest_ref[k * t_rows + t], 1)], sem)
                for k in range(2)]
    _row_copy_loop(ts, copies)


def _scatter_call(dest, h, n_rows, ts=512):
    t_rows, d = h.shape
    grid_spec = pltpu.PrefetchScalarGridSpec(
        num_scalar_prefetch=1, grid=(t_rows // ts,),
        in_specs=[pl.BlockSpec(memory_space=pl.ANY), pl.BlockSpec(memory_space=pl.ANY)],
        out_specs=pl.BlockSpec(memory_space=pl.ANY),
        scratch_shapes=[pltpu.SemaphoreType.DMA(())])
    return pl.pallas_call(
        functools.partial(_scatter_kernel, ts=ts, t_rows=t_rows),
        grid_spec=grid_spec,
        out_shape=jax.ShapeDtypeStruct((n_rows, d), h.dtype),
        input_output_aliases={2: 0},
        compiler_params=_cparams(("arbitrary",)),
        name="moe_scatter",
    )(dest, h, jnp.zeros((n_rows, d), h.dtype))


def _moe_kernel(te_ref, nu_ref, xg_ref, w1_ref, w3_ref, w2_ref, y_ref, hb_scr, acc_scr):
    del te_ref
    i = pl.program_id(0)
    f = pl.program_id(1)

    @pl.when(i < nu_ref[0])
    def _():
        @pl.when(f == 0)
        def _():
            hb_scr[...] = xg_ref[...].astype(BF16)
            acc_scr[...] = jnp.zeros(acc_scr.shape, F32)

        _swiglu_step(hb_scr[...], w1_ref.at[0], w3_ref.at[0], w2_ref.at[0], acc_scr)

        @pl.when(f == pl.num_programs(1) - 1)
        def _():
            y_ref[...] = acc_scr[...]

    @pl.when((i >= nu_ref[0]) & (f == 0))
    def _():
        y_ref[...] = jnp.zeros(y_ref.shape, F32)


def _moe_call(tile_expert, n_used, xg, w1, w3, w2, tm, tf=512):
    n_rows, d = xg.shape
    ff = w1.shape[2]
    nf = ff // tf
    f_eff = lambda i, f, nu: jnp.where(i < nu[0], f, nf - 1)
    grid_spec = pltpu.PrefetchScalarGridSpec(
        num_scalar_prefetch=2, grid=(n_rows // tm, nf),
        in_specs=[pl.BlockSpec((tm, d), lambda i, f, te, nu: (i, 0)),
                  pl.BlockSpec((1, d, tf), lambda i, f, te, nu: (te[i], 0, f_eff(i, f, nu))),
                  pl.BlockSpec((1, d, tf), lambda i, f, te, nu: (te[i], 0, f_eff(i, f, nu))),
                  pl.BlockSpec((1, tf, d), lambda i, f, te, nu: (te[i], f_eff(i, f, nu), 0))],
        out_specs=pl.BlockSpec((tm, d), lambda i, f, te, nu: (i, 0)),
        scratch_shapes=[pltpu.VMEM((tm, d), BF16), pltpu.VMEM((tm, d), F32)])
    return pl.pallas_call(
        _moe_kernel,
        grid_spec=grid_spec,
        out_shape=jax.ShapeDtypeStruct((n_rows, d), F32),
        compiler_params=_cparams(("arbitrary", "arbitrary")),
        name="moe_ffn",
    )(tile_expert, n_used, xg, w1, w3, w2)


def _combine_kernel(dest_ref, info_ref, x_ref, gate_ref, y_hbm, o_ref, buf, sem, *, tc, t_rows):
    base = pl.program_id(0) * tc

    def copies(r):
        t = base + r
        return [pltpu.make_async_copy(y_hbm.at[pl.ds(dest_ref[k * t_rows + t], 1)], buf.at[k, pl.ds(r, 1)], sem)
                for k in range(2)]
    _row_copy_loop(tc, copies)
    info = info_ref[...]
    y = info[:, 2:3] * buf[0] + info[:, 3:4] * buf[1]
    o_ref[...] = x_ref[...] + gate_ref[0] * y


def _combine_call(dest, info, x2d, gate, y, seq, tc=256):
    t_rows, d = x2d.shape
    tpb = seq // tc
    grid_spec = pltpu.PrefetchScalarGridSpec(
        num_scalar_prefetch=1, grid=(t_rows // tc,),
        in_specs=[pl.BlockSpec((tc, LANES), lambda i, dst: (i, 0)),
                  pl.BlockSpec((tc, d), lambda i, dst: (i, 0)),
                  pl.BlockSpec((1, 1, d), lambda i, dst: (i // tpb, 0, 0)),
                  pl.BlockSpec(memory_space=pl.ANY)],
        out_specs=pl.BlockSpec((tc, d), lambda i, dst: (i, 0)),
        scratch_shapes=[pltpu.VMEM((2, tc, d), F32), pltpu.SemaphoreType.DMA(())])
    return pl.pallas_call(
        functools.partial(_combine_kernel, tc=tc, t_rows=t_rows),
        grid_spec=grid_spec,
        out_shape=jax.ShapeDtypeStruct((t_rows, d), F32),
        compiler_params=_cparams(("arbitrary",)),
        name="moe_combine",
    )(dest, info, x2d, gate, y)


def _moe_block(h_f32, info, x2d, gate, w1, w3, w2, seq):
    t_rows = x2d.shape[0]
    dest, tile_expert, n_used = _route_plan(info, MOE_TM)
    n_rows = 2 * t_rows + N_EXPERTS * MOE_TM
    xg = _scatter_call(dest, h_f32, n_rows)
    y = _moe_call(tile_expert, n_used, xg, w1, w3, w2, MOE_TM)
    return _combine_call(dest, info, x2d, gate, y, seq)


def _rope_table(n_tokens, head_dim):
    axis_dim = head_dim // 2
    inv_freq = ROPE_THETA ** (-jnp.arange(0, axis_dim, 2, dtype=F32) / axis_dim)
    t = jnp.arange(n_tokens, dtype=jnp.int32)
    row = (t // GRID_W).astype(F32)
    col = (t % GRID_W).astype(F32)
    ang_r = row[:, None] * inv_freq[None, :]
    ang_c = col[:, None] * inv_freq[None, :]
    cos = jnp.concatenate([jnp.cos(ang_r)] * 2 + [jnp.cos(ang_c)] * 2, axis=-1)
    sin = jnp.concatenate([-jnp.sin(ang_r), jnp.sin(ang_r), -jnp.sin(ang_c), jnp.sin(ang_c)], axis=-1)
    reps = LANES // head_dim
    return jnp.tile(cos, (1, reps)), jnp.tile(sin, (1, reps))


def _identity_rope_table(n_tokens):
    return jnp.ones((n_tokens, LANES), F32), jnp.zeros((n_tokens, LANES), F32)


def _na_bias_selectors(seq):
    rows = seq // GRID_W
    nblk = rows // NA_QROWS
    col = np.arange(GRID_W)
    c0 = np.clip(col - NA_WIN_C // 2, 0, GRID_W - NA_WIN_C)[:, None]
    col_ok = (col[None, :] >= c0) & (col[None, :] < c0 + NA_WIN_C)
    ic = col[None, :] - col[:, None] + NA_WIN_C - 1
    sel_c = (col_ok[..., None] & (ic[..., None] == np.arange(2 * NA_WIN_C - 1))).astype(np.float32)
    sel_r = np.zeros((3, NA_QROWS, NA_KROWS, 2 * NA_WIN_R - 1), np.float32)
    row_ok = np.zeros((3, NA_QROWS, NA_KROWS), bool)
    for v, blk in enumerate((0, 1, nblk - 1)):
        r_base = blk * NA_QROWS
        u0 = int(np.clip(r_base - NA_WIN_R // 2, 0, rows - NA_KROWS))
        for q in range(NA_QROWS):
            r = r_base + q
            r0 = int(np.clip(r - NA_WIN_R // 2, 0, rows - NA_WIN_R))
            for k in range(NA_KROWS):
                kr = u0 + k
                if r0 <= kr < r0 + NA_WIN_R:
                    row_ok[v, q, k] = True
                    sel_r[v, q, k, kr - r + NA_WIN_R - 1] = 1.0
    valid = row_ok[:, :, None, :, None] & col_ok[None, None, :, None, :]
    return sel_r, sel_c, valid.reshape(3, NA_TQ, NA_TK)


def _na_bias_table(rpb, seq):
    sel_r, sel_c, valid = _na_bias_selectors(seq)
    hp = lax.Precision.HIGHEST
    t_col = jnp.einsum('hdi,cxi->hdcx', rpb.astype(F32), sel_c, precision=hp)
    full = jnp.einsum('vqkd,hdcx->hvqckx', sel_r, t_col, precision=hp)
    full = full.reshape(rpb.shape[0], 3, NA_TQ, NA_TK)
    return jnp.where(valid[None], full, NEG)


def _sw_mask_table(seq):
    nblk = seq // SW_TQ
    out = []
    for blk in (0, 1, nblk - 1):
        start = int(np.clip(blk * SW_TQ - SW_WINDOW, 0, seq - SW_TK))
        qpos = (blk * SW_TQ + np.arange(SW_TQ))[:, None]
        kpos = (start + np.arange(SW_TK))[None, :]
        out.append(np.where(np.abs(qpos - kpos) <= SW_WINDOW, 0.0, NEG))
    return jnp.asarray(np.stack(out), F32)


def _col_gains(na_q, na_k, sw_q, sw_k, df_q, df_k):
    one = lambda n: jnp.ones((n,), F32)
    df_scale = DF_QK_DIM ** -0.5
    return jnp.concatenate([
        jnp.tile(na_q, NA_HEADS), jnp.tile(na_k, NA_HEADS), one(NA_HEADS * HEAD_DIM),
        jnp.tile(sw_q, SW_HEADS), jnp.tile(sw_k, SW_KV_HEADS), one(SW_KV_HEADS * HEAD_DIM),
        jnp.tile(df_q * df_scale, 2 * DF_HEADS), jnp.tile(df_k, 2 * DF_HEADS), one(DF_HEADS * DF_V_DIM),
    ]).astype(F32)[None, :]


def kernel(x, c, ctx, c_ctx, ada_w, ada_b, norm1_g, norm2_g, w_in, w_out, na_q_norm, na_k_norm, na_rpb,
           sw_q_norm, sw_k_norm, sw_sink, df_q_norm, df_k_norm, df_lambda, df_subln_g,
           ffn_w1, ffn_w3, ffn_w2, moe_router, moe_w1, moe_w3, moe_w2):
    b, seq, d = x.shape
    lc = ctx.shape[1]
    depth = ada_w.shape[0]
    assert d == D_MODEL and seq % (GRID_W * NA_QROWS) == 0 and seq % DF_TK == 0

    c_rows = jnp.concatenate([c, c_ctx[None], jnp.zeros((16 - b - 1, d), F32)], axis=0)
    mod = _ada_call(c_rows, ada_w, ada_b)

    tabs_x = _rope_table(seq, HEAD_DIM) + _rope_table(seq, DF_QK_DIM)
    tabs_c = _identity_rope_table(lc) * 2
    sw_mask = _sw_mask_table(seq)

    x2 = x.reshape(b * seq, d)
    xc2 = ctx.reshape(b * lc, d)
    tm_x = 512
    tm_c = lc

    for i in range(depth):
        last = i == depth - 1
        m = mod[i].reshape(16, 6, d)
        mx = [m[:b, j][:, None, :] for j in range(6)]
        mc = [jnp.broadcast_to(m[b:b + 1, j][:, None, :], (b, 1, d)) for j in range(6)]
        lam_init = 0.8 - 0.6 * math.exp(-0.3 * i)
        gains = _col_gains(na_q_norm[i], na_k_norm[i], sw_q_norm[i], sw_k_norm[i], df_q_norm[i], df_k_norm[i])
        g1 = norm1_g[i][None, :]
        g2 = norm2_g[i][None, :]
        w_in_b = w_in[i].astype(BF16)
        w_out_b = w_out[i].astype(BF16)
        sink = sw_sink[i].astype(F32)
        lam_p = df_lambda[i][None].astype(F32)
        subln = df_subln_g[i][None, :]

        p = _proj_call(x2, mx[0], mx[1], g1, w_in_b, gains, tabs_x, seq, 1024).reshape(b, seq, IN_WIDTH)
        pc = _proj_call(xc2, mc[0], mc[1], g1, w_in_b, gains, tabs_c, lc, tm_c).reshape(b, lc, IN_WIDTH)
        oa = _na_call(p, pc, _na_bias_table(na_rpb[i], seq))
        ob = _sw_call(p, pc, sink, sw_mask)
        oc = _df_call(p, p, pc, lam_p, subln, lam_init, True)

        j = i // 2
        moe = i % 2 == 1
        if moe:
            r = jnp.pad(moe_router[j].astype(F32), ((0, 0), (0, LANES - N_EXPERTS)))
            r_hi = r.astype(BF16)
            r_lo = (r - r_hi.astype(F32)).astype(BF16)
            router = (r_hi, r_lo)
            w1, w3, w2 = moe_w1[j].astype(BF16), moe_w3[j].astype(BF16), moe_w2[j].astype(BF16)
        else:
            router = None
            w1, w3, w2 = ffn_w1[j].astype(BF16), ffn_w3[j].astype(BF16), ffn_w2[j].astype(BF16)

        def mix_and_ffn(oa_, ob_, oc_, xs, mv, seq_, tm_):
            flat = lambda o: o.reshape(xs.shape[0], o.shape[-1])
            res = _outproj_call(flat(oa_), flat(ob_), flat(oc_), w_out_b, xs, mv[2], g2, mv[3], mv[4],
                                seq_, min(tm_, 256), router)
            if moe:
                return _moe_block(res[1], res[2], res[0], mv[5], w1, w3, w2, seq_)
            return _ffn_call(res[1], w1, w3, w2, res[0], mv[5], seq_, tm_)

        if not last:
            coa = _ctx_call(pc, sink, COL_QA, COL_KA, COL_VA, NA_HEADS, 1, False)
            cob = _ctx_call(pc, sink, COL_QB, COL_KB, COL_VB, SW_KV_HEADS, SW_GROUP, True)
            coc = _df_call(pc, None, pc, lam_p, subln, lam_init, False)
            xc2 = mix_and_ffn(coa, cob, coc, xc2, mc, lc, tm_c)
        x2 = mix_and_ffn(oa, ob, oc, x2, mx, seq, tm_x)
    return x2.reshape(b, seq, d)
```

```python
import functools
import math

import numpy as np
import jax
import jax.numpy as jnp
from jax import lax
from jax.experimental import pallas as pl
from jax.experimental.pallas import tpu as pltpu

F32 = jnp.float32
BF16 = jnp.bfloat16

D_MODEL = 2048
GRID_W = 64
HEAD_DIM = 128
NA_HEADS = 6
NA_WIN_R = 8
NA_WIN_C = 16
SW_HEADS = 6
SW_KV_HEADS = 2
SW_GROUP = SW_HEADS // SW_KV_HEADS
SW_WINDOW = 128
DF_HEADS = 4
DF_QK_DIM = 64
DF_V_DIM = 128
IN_WIDTH = 5120
D_FF = 5632
N_EXPERTS = 8
ROPE_THETA = 10000.0
NORM_EPS = 1e-6
NEG = -1e30

LANES = 128
VMEM_LIMIT = 56 * 1024 * 1024

COL_QA, COL_KA, COL_VA = 0, 6, 12
COL_QB, COL_KB, COL_VB = 18, 24, 26
COL_QC, COL_KC, COL_VC = 28, 32, 36

T_PLAIN, T_NORM, T_NORM_ROPE_B, T_NORM_ROPE_C = 0, 1, 2, 3
CHUNK_TYPES = np.array([T_NORM] * 6 + [T_PLAIN] * 3 + [T_NORM_ROPE_B] * 4 + [T_PLAIN]
                       + [T_NORM_ROPE_C] * 4 + [T_PLAIN] * 2, dtype=np.int32)

NA_QROWS = 4
NA_KROWS = 12
NA_TQ = NA_QROWS * GRID_W
NA_TK = NA_KROWS * GRID_W
SW_TQ = 256
SW_TK = SW_TQ + 2 * SW_WINDOW
DF_TQ = 256
DF_TK = 512
DF_TILES = 2


def _cparams(sem):
    return pltpu.CompilerParams(dimension_semantics=sem, vmem_limit_bytes=VMEM_LIMIT)


def _dot(a, b):
    return jnp.dot(a, b, preferred_element_type=F32)


def _dot_nt(a, b):
    return lax.dot_general(a, b, (((1,), (1,)), ((), ())), preferred_element_type=F32)


def _ada_kernel(c_ref, w_ref, b_ref, o_ref):
    c = c_ref[...]
    sc = (c * jax.nn.sigmoid(c)).astype(BF16)
    o_ref[0] = _dot(sc, w_ref[0].astype(BF16)) + b_ref[0]


def _ada_call(c_rows, ada_w, ada_b):
    depth, d, n = ada_w.shape
    rows = c_rows.shape[0]
    tn = 1024
    return pl.pallas_call(
        _ada_kernel,
        grid=(depth, n // tn),
        in_specs=[pl.BlockSpec((rows, d), lambda l, j: (0, 0)),
                  pl.BlockSpec((1, d, tn), lambda l, j: (l, 0, j)),
                  pl.BlockSpec((1, 1, tn), lambda l, j: (l, 0, j))],
        out_specs=pl.BlockSpec((1, rows, tn), lambda l, j: (l, 0, j)),
        out_shape=jax.ShapeDtypeStruct((depth, rows, n), F32),
        compiler_params=_cparams(("arbitrary", "arbitrary")),
        name="ada",
    )(c_rows, ada_w, ada_b.reshape(depth, 1, n))


def _rms(v, gain):
    ms = jnp.mean(v * v, axis=-1, keepdims=True)
    return v * lax.rsqrt(ms + NORM_EPS) * gain


def _rms_half(v, gain, lane):
    lo = lane < DF_QK_DIM
    sq = v * v
    s_lo = jnp.sum(jnp.where(lo, sq, 0.0), axis=-1, keepdims=True)
    s_hi = jnp.sum(jnp.where(lo, 0.0, sq), axis=-1, keepdims=True)
    ms = jnp.where(lo, s_lo, s_hi) * (1.0 / DF_QK_DIM)
    return v * lax.rsqrt(ms + NORM_EPS) * gain


def _rope(v, cos, sin, half, lane):
    fwd = pltpu.roll(v, LANES - half, axis=1)
    bwd = pltpu.roll(v, half, axis=1)
    partner = jnp.where((lane & half) == 0, fwd, bwd)
    return v * cos + partner * sin


def _proj_kernel(types_ref, x_ref, shift_ref, scale_ref, g_ref, w_ref, gain_ref,
                 cb_ref, sb_ref, cc_ref, sc_ref, o_ref, h_scr, *, nch):
    j = pl.program_id(1)

    @pl.when(j == 0)
    def _():
        x = x_ref[...]
        h = _rms(x, g_ref[...]) * (1.0 + scale_ref[0]) + shift_ref[0]
        h_scr[...] = h.astype(BF16)

    p = _dot(h_scr[...], w_ref[...])
    tm = p.shape[0]
    lane = lax.broadcasted_iota(jnp.int32, (tm, LANES), 1)
    for s in range(nch):
        t = types_ref[j * nch + s]
        for hh in range(2):
            c0 = s * 256 + hh * LANES
            v = p[:, c0:c0 + LANES]
            gain = gain_ref[:, c0:c0 + LANES]

            @pl.when(t == T_PLAIN)
            def _():
                o_ref[:, c0:c0 + LANES] = v.astype(BF16)

            @pl.when(t == T_NORM)
            def _():
                o_ref[:, c0:c0 + LANES] = _rms(v, gain).astype(BF16)

            @pl.when(t == T_NORM_ROPE_B)
            def _():
                y = _rope(_rms(v, gain), cb_ref[...], sb_ref[...], HEAD_DIM // 4, lane)
                o_ref[:, c0:c0 + LANES] = y.astype(BF16)

            @pl.when(t == T_NORM_ROPE_C)
            def _():
                y = _rope(_rms_half(v, gain, lane), cc_ref[...], sc_ref[...], DF_QK_DIM // 4, lane)
                o_ref[:, c0:c0 + LANES] = y.astype(BF16)


def _proj_call(x2d, shift, scale, g, w, gains, tabs, seq, tm, tn=512):
    t_rows, d = x2d.shape
    n = w.shape[1]
    tpb = seq // tm
    nch = tn // 256
    grid_spec = pltpu.PrefetchScalarGridSpec(
        num_scalar_prefetch=1,
        grid=(t_rows // tm, n // tn),
        in_specs=[pl.BlockSpec((tm, d), lambda i, j, t: (i, 0)),
                  pl.BlockSpec((1, 1, d), lambda i, j, t: (i // tpb, 0, 0)),
                  pl.BlockSpec((1, 1, d), lambda i, j, t: (i // tpb, 0, 0)),
                  pl.BlockSpec((1, d), lambda i, j, t: (0, 0)),
                  pl.BlockSpec((d, tn), lambda i, j, t: (0, j)),
                  pl.BlockSpec((1, tn), lambda i, j, t: (0, j))]
                 + [pl.BlockSpec((tm, LANES), lambda i, j, t: (i % tpb, 0))] * 4,
        out_specs=pl.BlockSpec((tm, tn), lambda i, j, t: (i, j)),
        scratch_shapes=[pltpu.VMEM((tm, d), BF16)])
    return pl.pallas_call(
        functools.partial(_proj_kernel, nch=nch),
        grid_spec=grid_spec,
        out_shape=jax.ShapeDtypeStruct((t_rows, n), BF16),
        compiler_params=_cparams(("arbitrary", "arbitrary")),
        name="proj",
    )(jnp.asarray(CHUNK_TYPES), x2d, shift, scale, g, w, gains, *tabs)


def _softmax_parts(parts):
    m = parts[0].max(axis=-1, keepdims=True)
    for s in parts[1:]:
        m = jnp.maximum(m, s.max(axis=-1, keepdims=True))
    return m


def _na_kernel(q_ref, k_ref, v_ref, kc_ref, vc_ref, bias_ref, o_ref, *, nblk):
    blk = pl.program_id(2)
    u0 = jnp.clip(NA_QROWS * blk - NA_WIN_R // 2, 0, GRID_W - NA_KROWS)
    start = pl.multiple_of(u0 * GRID_W, GRID_W)
    var = jnp.where(blk == 0, 0, jnp.where(blk == nblk - 1, 2, 1))
    scale = HEAD_DIM ** -0.5
    q = q_ref[0]
    kw = k_ref[0, pl.ds(start, NA_TK), :]
    vw = v_ref[0, pl.ds(start, NA_TK), :]
    s_nb = _dot_nt(q, kw) * scale + bias_ref[0, var]
    s_cx = _dot_nt(q, kc_ref[0]) * scale
    m = jnp.maximum(s_nb.max(axis=-1, keepdims=True), s_cx.max(axis=-1, keepdims=True))
    p_nb = jnp.exp(s_nb - m)
    p_cx = jnp.exp(s_cx - m)
    l = p_nb.sum(axis=-1, keepdims=True) + p_cx.sum(axis=-1, keepdims=True)
    o = _dot(p_nb.astype(BF16), vw) + _dot(p_cx.astype(BF16), vc_ref[0])
    o_ref[0] = (o / l).astype(BF16)


def _na_call(p, pc, bias):
    b, seq, _ = p.shape
    lc = pc.shape[1]
    nblk = seq // NA_TQ
    return pl.pallas_call(
        functools.partial(_na_kernel, nblk=nblk),
        grid=(b, NA_HEADS, nblk),
        in_specs=[pl.BlockSpec((1, NA_TQ, LANES), lambda bi, h, i: (bi, i, COL_QA + h)),
                  pl.BlockSpec((1, seq, LANES), lambda bi, h, i: (bi, 0, COL_KA + h)),
                  pl.BlockSpec((1, seq, LANES), lambda bi, h, i: (bi, 0, COL_VA + h)),
                  pl.BlockSpec((1, lc, LANES), lambda bi, h, i: (bi, 0, COL_KA + h)),
                  pl.BlockSpec((1, lc, LANES), lambda bi, h, i: (bi, 0, COL_VA + h)),
                  pl.BlockSpec((1, 3, NA_TQ, NA_TK), lambda bi, h, i: (h, 0, 0, 0))],
        out_specs=pl.BlockSpec((1, NA_TQ, LANES), lambda bi, h, i: (bi, i, h)),
        out_shape=jax.ShapeDtypeStruct((b, seq, NA_HEADS * HEAD_DIM), BF16),
        compiler_params=_cparams(("arbitrary", "arbitrary", "arbitrary")),
        name="na_attn",
    )(p, p, p, pc, pc, bias)


def _sw_kernel(sink_ref, q_ref, k_ref, v_ref, kc_ref, vc_ref, mask_ref, o_ref, *, nblk, seq):
    hkv = pl.program_id(1)
    blk = pl.program_id(2)
    start = pl.multiple_of(jnp.clip(blk * SW_TQ - SW_WINDOW, 0, seq - SW_TK), SW_WINDOW)
    var = jnp.where(blk == 0, 0, jnp.where(blk == nblk - 1, 2, 1))
    scale = HEAD_DIM ** -0.5
    kw = k_ref[0, pl.ds(start, SW_TK), :]
    vw = v_ref[0, pl.ds(start, SW_TK), :]
    kc = kc_ref[0]
    vc = vc_ref[0]
    mask = mask_ref[var]
    for g in range(SW_GROUP):
        q = q_ref[0, :, g * LANES:(g + 1) * LANES]
        sink = sink_ref[hkv * SW_GROUP + g]
        s_w = _dot_nt(q, kw) * scale + mask
        s_c = _dot_nt(q, kc) * scale
        m = jnp.maximum(s_w.max(axis=-1, keepdims=True), s_c.max(axis=-1, keepdims=True))
        m = jnp.maximum(m, sink)
        p_w = jnp.exp(s_w - m)
        p_c = jnp.exp(s_c - m)
        l = p_w.sum(axis=-1, keepdims=True) + p_c.sum(axis=-1, keepdims=True) + jnp.exp(sink - m)
        o = _dot(p_w.astype(BF16), vw) + _dot(p_c.astype(BF16), vc)
        o_ref[0, :, g * LANES:(g + 1) * LANES] = (o / l).astype(BF16)


def _sw_call(p, pc, sink, mask):
    b, seq, _ = p.shape
    lc = pc.shape[1]
    nblk = seq // SW_TQ
    gw = SW_GROUP * LANES
    grid_spec = pltpu.PrefetchScalarGridSpec(
        num_scalar_prefetch=1,
        grid=(b, SW_KV_HEADS, nblk),
        in_specs=[pl.BlockSpec((1, SW_TQ, gw), lambda bi, h, i, s: (bi, i, COL_QB // SW_GROUP + h)),
                  pl.BlockSpec((1, seq, LANES), lambda bi, h, i, s: (bi, 0, COL_KB + h)),
                  pl.BlockSpec((1, seq, LANES), lambda bi, h, i, s: (bi, 0, COL_VB + h)),
                  pl.BlockSpec((1, lc, LANES), lambda bi, h, i, s: (bi, 0, COL_KB + h)),
                  pl.BlockSpec((1, lc, LANES), lambda bi, h, i, s: (bi, 0, COL_VB + h)),
                  pl.BlockSpec((3, SW_TQ, SW_TK), lambda bi, h, i, s: (0, 0, 0))],
        out_specs=pl.BlockSpec((1, SW_TQ, gw), lambda bi, h, i, s: (bi, i, h)))
    return pl.pallas_call(
        functools.partial(_sw_kernel, nblk=nblk, seq=seq),
        grid_spec=grid_spec,
        out_shape=jax.ShapeDtypeStruct((b, seq, SW_HEADS * HEAD_DIM), BF16),
        compiler_params=_cparams(("arbitrary", "arbitrary", "arbitrary")),
        name="sw_attn",
    )(sink, p, p, p, pc, pc, mask)


def _df_lambda(lam_ref, lam_init):
    lp = lam_ref[0]
    s1 = jnp.sum(jnp.sum(lp[0:1] * lp[1:2], axis=-1, keepdims=True), axis=0, keepdims=True)
    s2 = jnp.sum(jnp.sum(lp[2:3] * lp[3:4], axis=-1, keepdims=True), axis=0, keepdims=True)
    return jnp.exp(s1) - jnp.exp(s2) + lam_init


def _df_kernel(lam_ref, g_ref, q_ref, *refs, lam_init, with_latent, tq):
    if with_latent:
        k_ref, v_ref, kc_ref, vc_ref, o_ref = refs
    else:
        kc_ref, vc_ref, o_ref = refs
    chunks = []
    if with_latent:
        for c in range(k_ref.shape[1] // DF_TK):
            chunks.append((k_ref[0, c * DF_TK:(c + 1) * DF_TK, :], v_ref[0, c * DF_TK:(c + 1) * DF_TK, :]))
    chunks.append((kc_ref[0], vc_ref[0]))
    max_chunks = [kc_ref[0]]
    if with_latent:
        max_chunks += [k_ref[0, c * 2 * DF_TK:(c + 1) * 2 * DF_TK, :] for c in range(k_ref.shape[1] // (2 * DF_TK))]
    lam = _df_lambda(lam_ref, lam_init)

    def lane_fold(x, op, acc=None):
        for j in range(x.shape[1] // LANES):
            blk = x[:, j * LANES:(j + 1) * LANES]
            acc = blk if acc is None else op(acc, blk)
        return acc

    def one_tile(q):
        lane = lax.broadcasted_iota(jnp.int32, q.shape, 1)
        zero = jnp.zeros_like(q)
        q2 = jnp.concatenate([jnp.where(lane < DF_QK_DIM, q, zero),
                              jnp.where(lane < DF_QK_DIM, zero, q)], axis=0)
        m_part = None
        for kb in max_chunks:
            m_part = lane_fold(_dot_nt(q2, kb), jnp.maximum, m_part)
        m = m_part.max(axis=-1, keepdims=True)
        l_part = None
        acc = jnp.zeros((2 * tq, DF_V_DIM), F32)
        for kb, vb in chunks:
            pr = jnp.exp(_dot_nt(q2, kb) - m)
            l_part = lane_fold(pr, jnp.add, l_part)
            acc = acc + _dot(pr.astype(BF16), vb)
        o2 = acc / l_part.sum(axis=-1, keepdims=True)
        o = o2[:tq] - lam * o2[tq:]
        return _rms(o, g_ref[...]) * (1.0 - lam_init)

    outs = [one_tile(q_ref[0, t * tq:(t + 1) * tq, :]) for t in range(q_ref.shape[1] // tq)]
    o_ref[0] = jnp.concatenate(outs, axis=0).astype(BF16)


def _df_call(pq, pk, pc, lam_p, subln_g, lam_init, with_latent):
    b, lq, _ = pq.shape
    lc = pc.shape[1]
    tq = min(DF_TQ, lq)
    bq = min(DF_TILES * tq, lq)
    in_specs = [pl.BlockSpec((1, 4, DF_QK_DIM), lambda bi, h, i: (0, 0, 0)),
                pl.BlockSpec((1, DF_V_DIM), lambda bi, h, i: (0, 0)),
                pl.BlockSpec((1, bq, LANES), lambda bi, h, i: (bi, i, COL_QC + h))]
    args = [lam_p, subln_g, pq]
    if with_latent:
        seq = pk.shape[1]
        in_specs += [pl.BlockSpec((1, seq, LANES), lambda bi, h, i: (bi, 0, COL_KC + h)),
                     pl.BlockSpec((1, seq, LANES), lambda bi, h, i: (bi, 0, COL_VC + h))]
        args += [pk, pk]
    in_specs += [pl.BlockSpec((1, lc, LANES), lambda bi, h, i: (bi, 0, COL_KC + h)),
                 pl.BlockSpec((1, lc, LANES), lambda bi, h, i: (bi, 0, COL_VC + h))]
    args += [pc, pc]
    return pl.pallas_call(
        functools.partial(_df_kernel, lam_init=lam_init, with_latent=with_latent, tq=tq),
        grid=(b, DF_HEADS, lq // bq),
        in_specs=in_specs,
        out_specs=pl.BlockSpec((1, bq, LANES), lambda bi, h, i: (bi, i, h)),
        out_shape=jax.ShapeDtypeStruct((b, lq, DF_HEADS * DF_V_DIM), BF16),
        compiler_params=_cparams(("arbitrary", "arbitrary", "arbitrary")),
        name="df_attn" if with_latent else "df_attn_ctx",
    )(*args)


def _ctx_kernel(sink_ref, q_ref, k_ref, v_ref, o_ref, *, group, has_sink):
    hkv = pl.program_id(1)
    scale = HEAD_DIM ** -0.5
    k = k_ref[0]
    v = v_ref[0]
    for g in range(group):
        q = q_ref[0, :, g * LANES:(g + 1) * LANES]
        s = _dot_nt(q, k) * scale
        m = s.max(axis=-1, keepdims=True)
        if has_sink:
            sink = sink_ref[hkv * group + g]
            m = jnp.maximum(m, sink)
        pr = jnp.exp(s - m)
        l = pr.sum(axis=-1, keepdims=True)
        if has_sink:
            l = l + jnp.exp(sink - m)
        o_ref[0, :, g * LANES:(g + 1) * LANES] = (_dot(pr.astype(BF16), v) / l).astype(BF16)


def _ctx_call(pc, sink, col_q, col_k, col_v, n_kv, group, has_sink):
    b, lc, _ = pc.shape
    gw = group * LANES
    grid_spec = pltpu.PrefetchScalarGridSpec(
        num_scalar_prefetch=1,
        grid=(b, n_kv),
        in_specs=[pl.BlockSpec((1, lc, gw), lambda bi, h, s: (bi, 0, col_q // group + h)),
                  pl.BlockSpec((1, lc, LANES), lambda bi, h, s: (bi, 0, col_k + h)),
                  pl.BlockSpec((1, lc, LANES), lambda bi, h, s: (bi, 0, col_v + h))],
        out_specs=pl.BlockSpec((1, lc, gw), lambda bi, h, s: (bi, 0, h)))
    return pl.pallas_call(
        functools.partial(_ctx_kernel, group=group, has_sink=has_sink),
        grid_spec=grid_spec,
        out_shape=jax.ShapeDtypeStruct((b, lc, n_kv * gw), BF16),
        compiler_params=_cparams(("arbitrary", "arbitrary")),
        name="ctx_attn",
    )(sink, pc, pc, pc)


def _outproj_kernel(oa_ref, ob_ref, oc_ref, wa_ref, wb_ref, wc_ref, x_ref, gate_ref, g_ref,
                    shift_ref, scale_ref, *refs, with_router):
    if with_router:
        rh_ref, rl_ref, xo_ref, ho_ref, go_ref = refs
    else:
        xo_ref, ho_ref = refs
    y = _dot(oa_ref[...], wa_ref[...]) + _dot(ob_ref[...], wb_ref[...]) + _dot(oc_ref[...], wc_ref[...])
    xn = x_ref[...] + gate_ref[0] * y
    xo_ref[...] = xn
    h = _rms(xn, g_ref[...]) * (1.0 + scale_ref[0]) + shift_ref[0]
    hb = h.astype(BF16)
    ho_ref[...] = h.astype(ho_ref.dtype)
    if with_router:
        hl = (h - hb.astype(F32)).astype(BF16)
        logits = _dot(hb, rh_ref[...]) + (_dot(hb, rl_ref[...]) + _dot(hl, rh_ref[...]))
        lane = lax.broadcasted_iota(jnp.int32, logits.shape, 1).astype(F32)
        logits = jnp.where(lane < N_EXPERTS, logits, NEG)
        v1 = logits.max(axis=-1, keepdims=True)
        i1 = jnp.where(logits == v1, lane, float(LANES)).min(axis=-1, keepdims=True)
        rest = jnp.where(lane == i1, NEG, logits)
        v2 = rest.max(axis=-1, keepdims=True)
        i2 = jnp.where(rest == v2, lane, float(LANES)).min(axis=-1, keepdims=True)
        e2 = jnp.exp(v2 - v1)
        den = 1.0 + e2
        go_ref[...] = (jnp.where(lane == 0.0, i1, 0.0) + jnp.where(lane == 1.0, i2, 0.0)
                       + jnp.where(lane == 2.0, 1.0 / den, 0.0) + jnp.where(lane == 3.0, e2 / den, 0.0))


def _outproj_call(oa, ob, oc, w_out, x2d, gate, g2, shift, scale, seq, tm, router=None):
    t_rows, d = x2d.shape
    tpb = seq // tm
    wa, wb, wc = oa.shape[1], ob.shape[1], oc.shape[1]
    assert wa == wb and (wa + wb) % wc == 0
    row = lambda i: (i, 0)
    fixed = lambda i: (0, 0)
    per_b = lambda i: (i // tpb, 0, 0)
    in_specs = [pl.BlockSpec((tm, wa), row), pl.BlockSpec((tm, wb), row), pl.BlockSpec((tm, wc), row),
                pl.BlockSpec((wa, d), lambda i: (0, 0)), pl.BlockSpec((wb, d), lambda i: (1, 0)),
                pl.BlockSpec((wc, d), lambda i: ((wa + wb) // wc, 0)),
                pl.BlockSpec((tm, d), row), pl.BlockSpec((1, 1, d), per_b), pl.BlockSpec((1, d), fixed),
                pl.BlockSpec((1, 1, d), per_b), pl.BlockSpec((1, 1, d), per_b)]
    args = [oa, ob, oc, w_out, w_out, w_out, x2d, gate, g2, shift, scale]
    out_specs = [pl.BlockSpec((tm, d), row), pl.BlockSpec((tm, d), row)]
    out_shape = [jax.ShapeDtypeStruct((t_rows, d), F32),
                 jax.ShapeDtypeStruct((t_rows, d), BF16 if router is None else F32)]
    if router is not None:
        in_specs += [pl.BlockSpec((d, LANES), fixed), pl.BlockSpec((d, LANES), fixed)]
        args += list(router)
        out_specs.append(pl.BlockSpec((tm, LANES), row))
        out_shape.append(jax.ShapeDtypeStruct((t_rows, LANES), F32))
    return pl.pallas_call(
        functools.partial(_outproj_kernel, with_router=router is not None),
        grid=(t_rows // tm,),
        in_specs=in_specs, out_specs=out_specs, out_shape=out_shape,
        compiler_params=_cparams(("arbitrary",)),
        name="outproj",
    )(*args)


def _swiglu_step(h, w1_ref, w3_ref, w2_ref, acc_scr):
    a = _dot(h, w1_ref[...])
    b = _dot(h, w3_ref[...])
    act = (a * jax.nn.sigmoid(a) * b).astype(BF16)
    acc_scr[...] += _dot(act, w2_ref[...])


def _ffn_kernel(h_ref, w1_ref, w3_ref, w2_ref, x_ref, gate_ref, o_ref, acc_scr):
    f = pl.program_id(1)

    @pl.when(f == 0)
    def _():
        acc_scr[...] = jnp.zeros(acc_scr.shape, F32)

    _swiglu_step(h_ref[...], w1_ref, w3_ref, w2_ref, acc_scr)

    @pl.when(f == pl.num_programs(1) - 1)
    def _():
        o_ref[...] = x_ref[...] + gate_ref[0] * acc_scr[...]


def _ffn_call(h, w1, w3, w2, x2d, gate, seq, tm, tf=512):
    t_rows, d = x2d.shape
    ff = w1.shape[1]
    tpb = seq // tm
    return pl.pallas_call(
        _ffn_kernel,
        grid=(t_rows // tm, ff // tf),
        in_specs=[pl.BlockSpec((tm, d), lambda i, f: (i, 0)),
                  pl.BlockSpec((d, tf), lambda i, f: (0, f)),
                  pl.BlockSpec((d, tf), lambda i, f: (0, f)),
                  pl.BlockSpec((tf, d), lambda i, f: (f, 0)),
                  pl.BlockSpec((tm, d), lambda i, f: (i, 0)),
                  pl.BlockSpec((1, 1, d), lambda i, f: (i // tpb, 0, 0))],
        out_specs=pl.BlockSpec((tm, d), lambda i, f: (i, 0)),
        out_shape=jax.ShapeDtypeStruct((t_rows, d), F32),
        scratch_shapes=[pltpu.VMEM((tm, d), F32)],
        compiler_params=_cparams(("arbitrary", "arbitrary")),
        name="ffn",
    )(h, w1, w3, w2, x2d, gate)


MOE_TM = 512
ROW_UNROLL = 8


def _route_plan(info, tm):
    t_rows = info.shape[0]
    e_flat = info[:, :2].astype(jnp.int32).T.reshape(-1)
    onehot = (e_flat[:, None] == jnp.arange(N_EXPERTS, dtype=jnp.int32)[None, :]).astype(jnp.int32)
    csum = jnp.cumsum(onehot, axis=0)
    counts = csum[-1]
    padded = ((counts + tm - 1) // tm) * tm
    pend = jnp.cumsum(padded)
    pstart = pend - padded
    dest = jnp.sum(onehot * (csum - 1 + pstart[None, :]), axis=1).astype(jnp.int32)
    n_tiles = (2 * t_rows) // tm + N_EXPERTS
    n_used = (pend[-1] // tm).astype(jnp.int32)
    tile_idx = jnp.arange(n_tiles, dtype=jnp.int32)
    owner = jnp.sum((tile_idx[:, None] * tm >= pend[None, :]).astype(jnp.int32), axis=1)
    last_owner = jnp.sum(((n_used - 1) * tm >= pend).astype(jnp.int32))
    tile_expert = jnp.where(tile_idx < n_used, owner, last_owner).astype(jnp.int32)
    return dest, tile_expert, n_used.reshape(1)


def _row_copy_loop(n_rows, start_one):
    def issue(r, carry):
        for cp in start_one(r):
            cp.start()
        return carry
    lax.fori_loop(0, n_rows, issue, 0, unroll=ROW_UNROLL)

    def drain(r, carry):
        for cp in start_one(r):
            cp.wait()
        return carry
    lax.fori_loop(0, n_rows, drain, 0, unroll=ROW_UNROLL)


def _scatter_kernel(dest_ref, h_ref, xg_in, xg_out, sem, *, ts, t_rows):
    del xg_in
    base = pl.program_id(0) * ts

    def copies(r):
        t = base + r
        return [pltpu.make_async_copy(h_ref.at[pl.ds(r, 1)], xg_out.at[pl.ds(dest_ref[k * t_rows + t], 1)], sem)
                for k in range(2)]
    _row_copy_loop(ts, copies)


def _scatter_call(dest, h, n_rows, ts=256):
    t_rows, d = h.shape
    grid_spec = pltpu.PrefetchScalarGridSpec(
        num_scalar_prefetch=1, grid=(t_rows // ts,),
        in_specs=[pl.BlockSpec((ts, d), lambda i, dst: (i, 0)), pl.BlockSpec(memory_space=pl.ANY)],
        out_specs=pl.BlockSpec(memory_space=pl.ANY),
        scratch_shapes=[pltpu.SemaphoreType.DMA(())])
    return pl.pallas_call(
        functools.partial(_scatter_kernel, ts=ts, t_rows=t_rows),
        grid_spec=grid_spec,
        out_shape=jax.ShapeDtypeStruct((n_rows, d), h.dtype),
        input_output_aliases={2: 0},
        compiler_params=_cparams(("arbitrary",)),
        name="moe_scatter",
    )(dest, h, jnp.zeros((n_rows, d), h.dtype))


def _moe_kernel(te_ref, nu_ref, xg_ref, w1_ref, w3_ref, w2_ref, y_ref, hb_scr, acc_scr):
    del te_ref
    i = pl.program_id(0)
    f = pl.program_id(1)

    @pl.when(i < nu_ref[0])
    def _():
        @pl.when(f == 0)
        def _():
            hb_scr[...] = xg_ref[...].astype(BF16)
            acc_scr[...] = jnp.zeros(acc_scr.shape, F32)

        _swiglu_step(hb_scr[...], w1_ref.at[0], w3_ref.at[0], w2_ref.at[0], acc_scr)

        @pl.when(f == pl.num_programs(1) - 1)
        def _():
            y_ref[...] = acc_scr[...]

    @pl.when((i >= nu_ref[0]) & (f == 0))
    def _():
        y_ref[...] = jnp.zeros(y_ref.shape, F32)


def _moe_call(tile_expert, n_used, xg, w1, w3, w2, tm, tf=512):
    n_rows, d = xg.shape
    ff = w1.shape[2]
    nf = ff // tf
    f_eff = lambda i, f, nu: jnp.where(i < nu[0], f, nf - 1)
    grid_spec = pltpu.PrefetchScalarGridSpec(
        num_scalar_prefetch=2, grid=(n_rows // tm, nf),
        in_specs=[pl.BlockSpec((tm, d), lambda i, f, te, nu: (i, 0)),
                  pl.BlockSpec((1, d, tf), lambda i, f, te, nu: (te[i], 0, f_eff(i, f, nu))),
                  pl.BlockSpec((1, d, tf), lambda i, f, te, nu: (te[i], 0, f_eff(i, f, nu))),
                  pl.BlockSpec((1, tf, d), lambda i, f, te, nu: (te[i], f_eff(i, f, nu), 0))],
        out_specs=pl.BlockSpec((tm, d), lambda i, f, te, nu: (i, 0)),
        scratch_shapes=[pltpu.VMEM((tm, d), BF16), pltpu.VMEM((tm, d), F32)])
    return pl.pallas_call(
        _moe_kernel,
        grid_spec=grid_spec,
        out_shape=jax.ShapeDtypeStruct((n_rows, d), F32),
        compiler_params=_cparams(("arbitrary", "arbitrary")),
        name="moe_ffn",
    )(tile_expert, n_used, xg, w1, w3, w2)


def _combine_kernel(dest_ref, info_ref, x_ref, gate_ref, y_hbm, o_ref, buf, sem, *, tc, t_rows):
    base = pl.program_id(0) * tc

    def copies(r):
        t = base + r
        return [pltpu.make_async_copy(y_hbm.at[pl.ds(dest_ref[k * t_rows + t], 1)], buf.at[k, pl.ds(r, 1)], sem)
                for k in range(2)]
    _row_copy_loop(tc, copies)
    info = info_ref[...]
    y = info[:, 2:3] * buf[0] + info[:, 3:4] * buf[1]
    o_ref[...] = x_ref[...] + gate_ref[0] * y


def _combine_call(dest, info, x2d, gate, y, seq, tc=256):
    t_rows, d = x2d.shape
    tpb = seq // tc
    grid_spec = pltpu.PrefetchScalarGridSpec(
        num_scalar_prefetch=1, grid=(t_rows // tc,),
        in_specs=[pl.BlockSpec((tc, LANES), lambda i, dst: (i, 0)),
                  pl.BlockSpec((tc, d), lambda i, dst: (i, 0)),
                  pl.BlockSpec((1, 1, d), lambda i, dst: (i // tpb, 0, 0)),
                  pl.BlockSpec(memory_space=pl.ANY)],
        out_specs=pl.BlockSpec((tc, d), lambda i, dst: (i, 0)),
        scratch_shapes=[pltpu.VMEM((2, tc, d), F32), pltpu.SemaphoreType.DMA(())])
    return pl.pallas_call(
        functools.partial(_combine_kernel, tc=tc, t_rows=t_rows),
        grid_spec=grid_spec,
        out_shape=jax.ShapeDtypeStruct((t_rows, d), F32),
        compiler_params=_cparams(("arbitrary",)),
        name="moe_combine",
    )(dest, info, x2d, gate, y)


def _moe_block(h_f32, info, x2d, gate, w1, w3, w2, seq):
    t_rows = x2d.shape[0]
    dest, tile_expert, n_used = _route_plan(info, MOE_TM)
    n_rows = 2 * t_rows + N_EXPERTS * MOE_TM
    xg = _scatter_call(dest, h_f32, n_rows)
    y = _moe_call(tile_expert, n_used, xg, w1, w3, w2, MOE_TM)
    return _combine_call(dest, info, x2d, gate, y, seq)


def _rope_table(n_tokens, head_dim):
    axis_dim = head_dim // 2
    inv_freq = ROPE_THETA ** (-jnp.arange(0, axis_dim, 2, dtype=F32) / axis_dim)
    t = jnp.arange(n_tokens, dtype=jnp.int32)
    row = (t // GRID_W).astype(F32)
    col = (t % GRID_W).astype(F32)
    ang_r = row[:, None] * inv_freq[None, :]
    ang_c = col[:, None] * inv_freq[None, :]
    cos = jnp.concatenate([jnp.cos(ang_r)] * 2 + [jnp.cos(ang_c)] * 2, axis=-1)
    sin = jnp.concatenate([-jnp.sin(ang_r), jnp.sin(ang_r), -jnp.sin(ang_c), jnp.sin(ang_c)], axis=-1)
    reps = LANES // head_dim
    return jnp.tile(cos, (1, reps)), jnp.tile(sin, (1, reps))


def _identity_rope_table(n_tokens):
    return jnp.ones((n_tokens, LANES), F32), jnp.zeros((n_tokens, LANES), F32)


def _na_bias_selectors(seq):
    rows = seq // GRID_W
    nblk = rows // NA_QROWS
    col = np.arange(GRID_W)
    c0 = np.clip(col - NA_WIN_C // 2, 0, GRID_W - NA_WIN_C)[:, None]
    col_ok = (col[None, :] >= c0) & (col[None, :] < c0 + NA_WIN_C)
    ic = col[None, :] - col[:, None] + NA_WIN_C - 1
    sel_c = (col_ok[..., None] & (ic[..., None] == np.arange(2 * NA_WIN_C - 1))).astype(np.float32)
    sel_r = np.zeros((3, NA_QROWS, NA_KROWS, 2 * NA_WIN_R - 1), np.float32)
    row_ok = np.zeros((3, NA_QROWS, NA_KROWS), bool)
    for v, blk in enumerate((0, 1, nblk - 1)):
        r_base = blk * NA_QROWS
        u0 = int(np.clip(r_base - NA_WIN_R // 2, 0, rows - NA_KROWS))
        for q in range(NA_QROWS):
            r = r_base + q
            r0 = int(np.clip(r - NA_WIN_R // 2, 0, rows - NA_WIN_R))
            for k in range(NA_KROWS):
                kr = u0 + k
                if r0 <= kr < r0 + NA_WIN_R:
                    row_ok[v, q, k] = True
                    sel_r[v, q, k, kr - r + NA_WIN_R - 1] = 1.0
    valid = row_ok[:, :, None, :, None] & col_ok[None, None, :, None, :]
    return sel_r, sel_c, valid.reshape(3, NA_TQ, NA_TK)


def _na_bias_table(rpb, seq):
    sel_r, sel_c, valid = _na_bias_selectors(seq)
    hp = lax.Precision.HIGHEST
    t_col = jnp.einsum('hdi,cxi->hdcx', rpb.astype(F32), sel_c, precision=hp)
    full = jnp.einsum('vqkd,hdcx->hvqckx', sel_r, t_col, precision=hp)
    full = full.reshape(rpb.shape[0], 3, NA_TQ, NA_TK)
    return jnp.where(valid[None], full, NEG)


def _sw_mask_table(seq):
    nblk = seq // SW_TQ
    out = []
    for blk in (0, 1, nblk - 1):
        start = int(np.clip(blk * SW_TQ - SW_WINDOW, 0, seq - SW_TK))
        qpos = (blk * SW_TQ + np.arange(SW_TQ))[:, None]
        kpos = (start + np.arange(SW_TK))[None, :]
        out.append(np.where(np.abs(qpos - kpos) <= SW_WINDOW, 0.0, NEG))
    return jnp.asarray(np.stack(out), F32)


def _col_gains(na_q, na_k, sw_q, sw_k, df_q, df_k):
    one = lambda n: jnp.ones((n,), F32)
    df_scale = DF_QK_DIM ** -0.5
    return jnp.concatenate([
        jnp.tile(na_q, NA_HEADS), jnp.tile(na_k, NA_HEADS), one(NA_HEADS * HEAD_DIM),
        jnp.tile(sw_q, SW_HEADS), jnp.tile(sw_k, SW_KV_HEADS), one(SW_KV_HEADS * HEAD_DIM),
        jnp.tile(df_q * df_scale, 2 * DF_HEADS), jnp.tile(df_k, 2 * DF_HEADS), one(DF_HEADS * DF_V_DIM),
    ]).astype(F32)[None, :]


def kernel(x, c, ctx, c_ctx, ada_w, ada_b, norm1_g, norm2_g, w_in, w_out, na_q_norm, na_k_norm, na_rpb,
           sw_q_norm, sw_k_norm, sw_sink, df_q_norm, df_k_norm, df_lambda, df_subln_g,
           ffn_w1, ffn_w3, ffn_w2, moe_router, moe_w1, moe_w3, moe_w2):
    b, seq, d = x.shape
    lc = ctx.shape[1]
    depth = ada_w.shape[0]
    assert d == D_MODEL and seq % (GRID_W * NA_QROWS) == 0 and seq % DF_TK == 0

    c_rows = jnp.concatenate([c, c_ctx[None], jnp.zeros((16 - b - 1, d), F32)], axis=0)
    mod = _ada_call(c_rows, ada_w, ada_b)

    tabs_x = _rope_table(seq, HEAD_DIM) + _rope_table(seq, DF_QK_DIM)
    tabs_c = _identity_rope_table(lc) * 2
    sw_mask = _sw_mask_table(seq)

    x2 = x.reshape(b * seq, d)
    xc2 = ctx.reshape(b * lc, d)
    tm_x = 512
    tm_c = lc

    for i in range(depth):
        last = i == depth - 1
        m = mod[i].reshape(16, 6, d)
        mx = [m[:b, j][:, None, :] for j in range(6)]
        mc = [jnp.broadcast_to(m[b:b + 1, j][:, None, :], (b, 1, d)) for j in range(6)]
        lam_init = 0.8 - 0.6 * math.exp(-0.3 * i)
        gains = _col_gains(na_q_norm[i], na_k_norm[i], sw_q_norm[i], sw_k_norm[i], df_q_norm[i], df_k_norm[i])
        g1 = norm1_g[i][None, :]
        g2 = norm2_g[i][None, :]
        w_in_b = w_in[i].astype(BF16)
        w_out_b = w_out[i].astype(BF16)
        sink = sw_sink[i].astype(F32)
        lam_p = df_lambda[i][None].astype(F32)
        subln = df_subln_g[i][None, :]

        p = _proj_call(x2, mx[0], mx[1], g1, w_in_b, gains, tabs_x, seq, 1024).reshape(b, seq, IN_WIDTH)
        pc = _proj_call(xc2, mc[0], mc[1], g1, w_in_b, gains, tabs_c, lc, tm_c).reshape(b, lc, IN_WIDTH)
        oa = _na_call(p, pc, _na_bias_table(na_rpb[i], seq))
        ob = _sw_call(p, pc, sink, sw_mask)
        oc = _df_call(p, p, pc, lam_p, subln, lam_init, True)

        j = i // 2
        moe = i % 2 == 1
        if moe:
            r = jnp.pad(moe_router[j].astype(F32), ((0, 0), (0, LANES - N_EXPERTS)))
            r_hi = r.astype(BF16)
            r_lo = (r - r_hi.astype(F32)).astype(BF16)
            router = (r_hi, r_lo)
            w1, w3, w2 = moe_w1[j].astype(BF16), moe_w3[j].astype(BF16), moe_w2[j].astype(BF16)
        else:
            router = None
            w1, w3, w2 = ffn_w1[j].astype(BF16), ffn_w3[j].astype(BF16), ffn_w2[j].astype(BF16)

        def mix_and_ffn(oa_, ob_, oc_, xs, mv, seq_, tm_):
            flat = lambda o: o.reshape(xs.shape[0], o.shape[-1])
            res = _outproj_call(flat(oa_), flat(ob_), flat(oc_), w_out_b, xs, mv[2], g2, mv[3], mv[4],
                                seq_, min(tm_, 256), router)
            if moe:
                return _moe_block(res[1], res[2], res[0], mv[5], w1, w3, w2, seq_)
            return _ffn_call(res[1], w1, w3, w2, res[0], mv[5], seq_, tm_)

        if not last:
            coa = _ctx_call(pc, sink, COL_QA, COL_KA, COL_VA, NA_HEADS, 1, False)
            cob = _ctx_call(pc, sink, COL_QB, COL_KB, COL_VB, SW_KV_HEADS, SW_GROUP, True)
            coc = _df_call(pc, None, pc, lam_p, subln, lam_init, False)
            xc2 = mix_and_ffn(coa, cob, coc, xc2, mc, lc, tm_c)
        x2 = mix_and_ffn(oa, ob, oc, x2, mx, seq, tm_x)
    return x2.reshape(b, seq, d)
```

```python
import functools
import math

import numpy as np
import jax
import jax.numpy as jnp
from jax import lax
from jax.experimental import pallas as pl
from jax.experimental.pallas import tpu as pltpu

F32 = jnp.float32
BF16 = jnp.bfloat16

D_MODEL = 2048
GRID_W = 64
HEAD_DIM = 128
NA_HEADS = 6
NA_WIN_R = 8
NA_WIN_C = 16
SW_HEADS = 6
SW_KV_HEADS = 2
SW_GROUP = SW_HEADS // SW_KV_HEADS
SW_WINDOW = 128
DF_HEADS = 4
DF_QK_DIM = 64
DF_V_DIM = 128
IN_WIDTH = 5120
D_FF = 5632
N_EXPERTS = 8
ROPE_THETA = 10000.0
NORM_EPS = 1e-6
NEG = -1e30

LANES = 128
VMEM_LIMIT = 56 * 1024 * 1024

COL_QA, COL_KA, COL_VA = 0, 6, 12
COL_QB, COL_KB, COL_VB = 18, 24, 26
COL_QC, COL_KC, COL_VC = 28, 32, 36

PROJ_CHUNK = 256
T_PLAIN, T_NORM, T_NORM_ROPE_B, T_NORM_ROPE_C = 0, 1, 2, 3
CHUNK_TYPES = ((T_NORM,) * 6 + (T_PLAIN,) * 3 + (T_NORM_ROPE_B,) * 4 + (T_PLAIN,)
               + (T_NORM_ROPE_C,) * 4 + (T_PLAIN,) * 2)

NA_QROWS = 4
NA_KROWS = 12
NA_TQ = NA_QROWS * GRID_W
NA_TK = NA_KROWS * GRID_W
NA_TILES = 2
SW_TQ = 256
SW_TK = SW_TQ + 2 * SW_WINDOW
DF_TQ = 256
DF_TK = 512
DF_TILES = 2


def _cparams(sem):
    return pltpu.CompilerParams(dimension_semantics=sem, vmem_limit_bytes=VMEM_LIMIT)


def _dot(a, b):
    return jnp.dot(a, b, preferred_element_type=F32)


def _dot_nt(a, b):
    return lax.dot_general(a, b, (((1,), (1,)), ((), ())), preferred_element_type=F32)


def _ada_kernel(c_ref, w_ref, b_ref, o_ref):
    c = c_ref[...]
    sc = (c * jax.nn.sigmoid(c)).astype(BF16)
    o_ref[0] = _dot(sc, w_ref[0].astype(BF16)) + b_ref[0]


def _ada_call(c_rows, ada_w, ada_b):
    depth, d, n = ada_w.shape
    rows = c_rows.shape[0]
    tn = 1024
    return pl.pallas_call(
        _ada_kernel,
        grid=(depth, n // tn),
        in_specs=[pl.BlockSpec((rows, d), lambda l, j: (0, 0)),
                  pl.BlockSpec((1, d, tn), lambda l, j: (l, 0, j)),
                  pl.BlockSpec((1, 1, tn), lambda l, j: (l, 0, j))],
        out_specs=pl.BlockSpec((1, rows, tn), lambda l, j: (l, 0, j)),
        out_shape=jax.ShapeDtypeStruct((depth, rows, n), F32),
        compiler_params=_cparams(("arbitrary", "arbitrary")),
        name="ada",
    )(c_rows, ada_w, ada_b.reshape(depth, 1, n))


def _rms(v, gain):
    ms = jnp.mean(v * v, axis=-1, keepdims=True)
    return v * lax.rsqrt(ms + NORM_EPS) * gain


def _rms_half(v, gain, lane):
    lo = lane < DF_QK_DIM
    sq = v * v
    s_lo = jnp.sum(jnp.where(lo, sq, 0.0), axis=-1, keepdims=True)
    s_hi = jnp.sum(jnp.where(lo, 0.0, sq), axis=-1, keepdims=True)
    ms = jnp.where(lo, s_lo, s_hi) * (1.0 / DF_QK_DIM)
    return v * lax.rsqrt(ms + NORM_EPS) * gain


def _rope(v, cos, sin, half, lane):
    fwd = pltpu.roll(v, LANES - half, axis=1)
    bwd = pltpu.roll(v, half, axis=1)
    partner = jnp.where((lane & half) == 0, fwd, bwd)
    return v * cos + partner * sin


def _proj_kernel(x_ref, shift_ref, scale_ref, g_ref, w_ref, gain_ref,
                 cb_ref, sb_ref, cc_ref, sc_ref, o_ref):
    h = (_rms(x_ref[...], g_ref[...]) * (1.0 + scale_ref[0]) + shift_ref[0]).astype(BF16)
    lane = lax.broadcasted_iota(jnp.int32, (h.shape[0], LANES), 1)
    for c, t in enumerate(CHUNK_TYPES):
        p = _dot(h, w_ref[:, c * PROJ_CHUNK:(c + 1) * PROJ_CHUNK])
        for hh in range(PROJ_CHUNK // LANES):
            c0 = c * PROJ_CHUNK + hh * LANES
            v = p[:, hh * LANES:(hh + 1) * LANES]
            gain = gain_ref[:, c0:c0 + LANES]
            if t == T_NORM:
                v = _rms(v, gain)
            elif t == T_NORM_ROPE_B:
                v = _rope(_rms(v, gain), cb_ref[...], sb_ref[...], HEAD_DIM // 4, lane)
            elif t == T_NORM_ROPE_C:
                v = _rope(_rms_half(v, gain, lane), cc_ref[...], sc_ref[...], DF_QK_DIM // 4, lane)
            o_ref[:, c0:c0 + LANES] = v.astype(BF16)


def _proj_call(x2d, shift, scale, g, w, gains, tabs, seq, tm):
    t_rows, d = x2d.shape
    n = w.shape[1]
    assert n == PROJ_CHUNK * len(CHUNK_TYPES)
    tpb = seq // tm
    return pl.pallas_call(
        _proj_kernel,
        grid=(t_rows // tm,),
        in_specs=[pl.BlockSpec((tm, d), lambda i: (i, 0)),
                  pl.BlockSpec((1, 1, d), lambda i: (i // tpb, 0, 0)),
                  pl.BlockSpec((1, 1, d), lambda i: (i // tpb, 0, 0)),
                  pl.BlockSpec((1, d), lambda i: (0, 0)),
                  pl.BlockSpec((d, n), lambda i: (0, 0), pipeline_mode=pl.Buffered(1)),
                  pl.BlockSpec((1, n), lambda i: (0, 0))]
                 + [pl.BlockSpec((tm, LANES), lambda i: (i % tpb, 0))] * 4,
        out_specs=pl.BlockSpec((tm, n), lambda i: (i, 0)),
        out_shape=jax.ShapeDtypeStruct((t_rows, n), BF16),
        compiler_params=_cparams(("arbitrary",)),
        name="proj",
    )(x2d, shift, scale, g, w, gains, *tabs)


def _softmax_parts(parts):
    m = parts[0].max(axis=-1, keepdims=True)
    for s in parts[1:]:
        m = jnp.maximum(m, s.max(axis=-1, keepdims=True))
    return m


def _na_kernel(q_ref, k_ref, v_ref, kc_ref, vc_ref, bias_ref, o_ref, *, nblk):
    scale = HEAD_DIM ** -0.5

    def one_block(t):
        blk = pl.program_id(2) * NA_TILES + t
        u0 = jnp.clip(NA_QROWS * blk - NA_WIN_R // 2, 0, GRID_W - NA_KROWS)
        start = pl.multiple_of(u0 * GRID_W, GRID_W)
        var = jnp.where(blk == 0, 0, jnp.where(blk == nblk - 1, 2, 1))
        q = q_ref[0, t * NA_TQ:(t + 1) * NA_TQ, :]
        kw = k_ref[0, pl.ds(start, NA_TK), :]
        vw = v_ref[0, pl.ds(start, NA_TK), :]
        s_nb = _dot_nt(q, kw) * scale + bias_ref[0, var]
        s_cx = _dot_nt(q, kc_ref[0]) * scale
        m = jnp.maximum(s_nb.max(axis=-1, keepdims=True), s_cx.max(axis=-1, keepdims=True))
        p_nb = jnp.exp(s_nb - m)
        p_cx = jnp.exp(s_cx - m)
        l = p_nb.sum(axis=-1, keepdims=True) + p_cx.sum(axis=-1, keepdims=True)
        return (_dot(p_nb.astype(BF16), vw) + _dot(p_cx.astype(BF16), vc_ref[0])) / l

    o_ref[0] = jnp.concatenate([one_block(t) for t in range(NA_TILES)], axis=0).astype(BF16)


def _na_call(p, pc, bias):
    b, seq, _ = p.shape
    lc = pc.shape[1]
    nblk = seq // NA_TQ
    bq = NA_TILES * NA_TQ
    return pl.pallas_call(
        functools.partial(_na_kernel, nblk=nblk),
        grid=(b, NA_HEADS, seq // bq),
        in_specs=[pl.BlockSpec((1, bq, LANES), lambda bi, h, i: (bi, i, COL_QA + h)),
                  pl.BlockSpec((1, seq, LANES), lambda bi, h, i: (bi, 0, COL_KA + h)),
                  pl.BlockSpec((1, seq, LANES), lambda bi, h, i: (bi, 0, COL_VA + h)),
                  pl.BlockSpec((1, lc, LANES), lambda bi, h, i: (bi, 0, COL_KA + h)),
                  pl.BlockSpec((1, lc, LANES), lambda bi, h, i: (bi, 0, COL_VA + h)),
                  pl.BlockSpec((1, 3, NA_TQ, NA_TK), lambda bi, h, i: (h, 0, 0, 0))],
        out_specs=pl.BlockSpec((1, bq, LANES), lambda bi, h, i: (bi, i, h)),
        out_shape=jax.ShapeDtypeStruct((b, seq, NA_HEADS * HEAD_DIM), BF16),
        compiler_params=_cparams(("arbitrary", "arbitrary", "arbitrary")),
        name="na_attn",
    )(p, p, p, pc, pc, bias)


def _sw_kernel(sink_ref, q_ref, k_ref, v_ref, kc_ref, vc_ref, mask_ref, o_ref, *, nblk, seq):
    hkv = pl.program_id(1)
    blk = pl.program_id(2)
    start = pl.multiple_of(jnp.clip(blk * SW_TQ - SW_WINDOW, 0, seq - SW_TK), SW_WINDOW)
    var = jnp.where(blk == 0, 0, jnp.where(blk == nblk - 1, 2, 1))
    scale = HEAD_DIM ** -0.5
    kw = k_ref[0, pl.ds(start, SW_TK), :]
    vw = v_ref[0, pl.ds(start, SW_TK), :]
    kc = kc_ref[0]
    vc = vc_ref[0]
    mask = mask_ref[var]
    outs = []
    for g in range(SW_GROUP):
        q = q_ref[0, :, g * LANES:(g + 1) * LANES]
        sink = sink_ref[hkv * SW_GROUP + g]
        s_w = _dot_nt(q, kw) * scale + mask
        s_c = _dot_nt(q, kc) * scale
        m = jnp.maximum(s_w.max(axis=-1, keepdims=True), s_c.max(axis=-1, keepdims=True))
        m = jnp.maximum(m, sink)
        p_w = jnp.exp(s_w - m)
        p_c = jnp.exp(s_c - m)
        l = p_w.sum(axis=-1, keepdims=True) + p_c.sum(axis=-1, keepdims=True) + jnp.exp(sink - m)
        o = _dot(p_w.astype(BF16), vw) + _dot(p_c.astype(BF16), vc)
        outs.append(o / l)
    o_ref[0] = jnp.concatenate(outs, axis=1).astype(BF16)


def _sw_call(p, pc, sink, mask):
    b, seq, _ = p.shape
    lc = pc.shape[1]
    nblk = seq // SW_TQ
    gw = SW_GROUP * LANES
    grid_spec = pltpu.PrefetchScalarGridSpec(
        num_scalar_prefetch=1,
        grid=(b, SW_KV_HEADS, nblk),
        in_specs=[pl.BlockSpec((1, SW_TQ, gw), lambda bi, h, i, s: (bi, i, COL_QB // SW_GROUP + h)),
                  pl.BlockSpec((1, seq, LANES), lambda bi, h, i, s: (bi, 0, COL_KB + h)),
                  pl.BlockSpec((1, seq, LANES), lambda bi, h, i, s: (bi, 0, COL_VB + h)),
                  pl.BlockSpec((1, lc, LANES), lambda bi, h, i, s: (bi, 0, COL_KB + h)),
                  pl.BlockSpec((1, lc, LANES), lambda bi, h, i, s: (bi, 0, COL_VB + h)),
                  pl.BlockSpec((3, SW_TQ, SW_TK), lambda bi, h, i, s: (0, 0, 0))],
        out_specs=pl.BlockSpec((1, SW_TQ, gw), lambda bi, h, i, s: (bi, i, h)))
    return pl.pallas_call(
        functools.partial(_sw_kernel, nblk=nblk, seq=seq),
        grid_spec=grid_spec,
        out_shape=jax.ShapeDtypeStruct((b, seq, SW_HEADS * HEAD_DIM), BF16),
        compiler_params=_cparams(("arbitrary", "arbitrary", "arbitrary")),
        name="sw_attn",
    )(sink, p, p, p, pc, pc, mask)


def _df_lambda(lam_ref, lam_init):
    lp = lam_ref[0]
    s1 = jnp.sum(jnp.sum(lp[0:1] * lp[1:2], axis=-1, keepdims=True), axis=0, keepdims=True)
    s2 = jnp.sum(jnp.sum(lp[2:3] * lp[3:4], axis=-1, keepdims=True), axis=0, keepdims=True)
    return jnp.exp(s1) - jnp.exp(s2) + lam_init


def _df_kernel(lam_ref, g_ref, q_ref, *refs, lam_init, with_latent, tq):
    if with_latent:
        k_ref, v_ref, kc_ref, vc_ref, o_ref = refs
    else:
        kc_ref, vc_ref, o_ref = refs
    chunks = []
    if with_latent:
        for c in range(k_ref.shape[1] // DF_TK):
            chunks.append((k_ref[0, c * DF_TK:(c + 1) * DF_TK, :], v_ref[0, c * DF_TK:(c + 1) * DF_TK, :]))
    chunks.append((kc_ref[0], vc_ref[0]))
    max_chunks = [kc_ref[0]]
    if with_latent:
        max_chunks += [k_ref[0, c * 2 * DF_TK:(c + 1) * 2 * DF_TK, :] for c in range(k_ref.shape[1] // (2 * DF_TK))]
    lam = _df_lambda(lam_ref, lam_init)

    def lane_fold(x, op, acc=None):
        for j in range(x.shape[1] // LANES):
            blk = x[:, j * LANES:(j + 1) * LANES]
            acc = blk if acc is None else op(acc, blk)
        return acc

    def one_tile(q):
        lane = lax.broadcasted_iota(jnp.int32, q.shape, 1)
        zero = jnp.zeros_like(q)
        q2 = jnp.concatenate([jnp.where(lane < DF_QK_DIM, q, zero),
                              jnp.where(lane < DF_QK_DIM, zero, q)], axis=0)
        m_part = None
        for kb in max_chunks:
            m_part = lane_fold(_dot_nt(q2, kb), jnp.maximum, m_part)
        m = m_part.max(axis=-1, keepdims=True)
        l_part = None
        acc = jnp.zeros((2 * tq, DF_V_DIM), F32)
        for kb, vb in chunks:
            pr = jnp.exp(_dot_nt(q2, kb) - m)
            l_part = lane_fold(pr, jnp.add, l_part)
            acc = acc + _dot(pr.astype(BF16), vb)
        o2 = acc / l_part.sum(axis=-1, keepdims=True)
        o = o2[:tq] - lam * o2[tq:]
        return _rms(o, g_ref[...]) * (1.0 - lam_init)

    outs = [one_tile(q_ref[0, t * tq:(t + 1) * tq, :]) for t in range(q_ref.shape[1] // tq)]
    o_ref[0] = jnp.concatenate(outs, axis=0).astype(BF16)


def _df_call(pq, pk, pc, lam_p, subln_g, lam_init, with_latent):
    b, lq, _ = pq.shape
    lc = pc.shape[1]
    tq = min(DF_TQ, lq)
    bq = min(DF_TILES * tq, lq)
    in_specs = [pl.BlockSpec((1, 4, DF_QK_DIM), lambda bi, h, i: (0, 0, 0)),
                pl.BlockSpec((1, DF_V_DIM), lambda bi, h, i: (0, 0)),
                pl.BlockSpec((1, bq, LANES), lambda bi, h, i: (bi, i, COL_QC + h))]
    args = [lam_p, subln_g, pq]
    if with_latent:
        seq = pk.shape[1]
        in_specs += [pl.BlockSpec((1, seq, LANES), lambda bi, h, i: (bi, 0, COL_KC + h)),
                     pl.BlockSpec((1, seq, LANES), lambda bi, h, i: (bi, 0, COL_VC + h))]
        args += [pk, pk]
    in_specs += [pl.BlockSpec((1, lc, LANES), lambda bi, h, i: (bi, 0, COL_KC + h)),
                 pl.BlockSpec((1, lc, LANES), lambda bi, h, i: (bi, 0, COL_VC + h))]
    args += [pc, pc]
    return pl.pallas_call(
        functools.partial(_df_kernel, lam_init=lam_init, with_latent=with_latent, tq=tq),
        grid=(b, DF_HEADS, lq // bq),
        in_specs=in_specs,
        out_specs=pl.BlockSpec((1, bq, LANES), lambda bi, h, i: (bi, i, h)),
        out_shape=jax.ShapeDtypeStruct((b, lq, DF_HEADS * DF_V_DIM), BF16),
        compiler_params=_cparams(("arbitrary", "arbitrary", "arbitrary")),
        name="df_attn" if with_latent else "df_attn_ctx",
    )(*args)


def _ctx_kernel(sink_ref, q_ref, k_ref, v_ref, o_ref, *, group, has_sink):
    hkv = pl.program_id(1)
    scale = HEAD_DIM ** -0.5
    k = k_ref[0]
    v = v_ref[0]
    for g in range(group):
        q = q_ref[0, :, g * LANES:(g + 1) * LANES]
        s = _dot_nt(q, k) * scale
        m = s.max(axis=-1, keepdims=True)
        if has_sink:
            sink = sink_ref[hkv * group + g]
            m = jnp.maximum(m, sink)
        pr = jnp.exp(s - m)
        l = pr.sum(axis=-1, keepdims=True)
        if has_sink:
            l = l + jnp.exp(sink - m)
        o_ref[0, :, g * LANES:(g + 1) * LANES] = (_dot(pr.astype(BF16), v) / l).astype(BF16)


def _ctx_call(pc, sink, col_q, col_k, col_v, n_kv, group, has_sink):
    b, lc, _ = pc.shape
    gw = group * LANES
    grid_spec = pltpu.PrefetchScalarGridSpec(
        num_scalar_prefetch=1,
        grid=(b, n_kv),
        in_specs=[pl.BlockSpec((1, lc, gw), lambda bi, h, s: (bi, 0, col_q // group + h)),
                  pl.BlockSpec((1, lc, LANES), lambda bi, h, s: (bi, 0, col_k + h)),
                  pl.BlockSpec((1, lc, LANES), lambda bi, h, s: (bi, 0, col_v + h))],
        out_specs=pl.BlockSpec((1, lc, gw), lambda bi, h, s: (bi, 0, h)))
    return pl.pallas_call(
        functools.partial(_ctx_kernel, group=group, has_sink=has_sink),
        grid_spec=grid_spec,
        out_shape=jax.ShapeDtypeStruct((b, lc, n_kv * gw), BF16),
        compiler_params=_cparams(("arbitrary", "arbitrary")),
        name="ctx_attn",
    )(sink, pc, pc, pc)


def _outproj_kernel(oa_ref, ob_ref, oc_ref, wa_ref, wb_ref, wc_ref, x_ref, gate_ref, g_ref,
                    shift_ref, scale_ref, *refs, with_router):
    if with_router:
        rh_ref, rl_ref, xo_ref, ho_ref, go_ref = refs
    else:
        xo_ref, ho_ref = refs
    y = _dot(oa_ref[...], wa_ref[...]) + _dot(ob_ref[...], wb_ref[...]) + _dot(oc_ref[...], wc_ref[...])
    xn = x_ref[...] + gate_ref[0] * y
    xo_ref[...] = xn
    h = _rms(xn, g_ref[...]) * (1.0 + scale_ref[0]) + shift_ref[0]
    hb = h.astype(BF16)
    ho_ref[...] = h.astype(ho_ref.dtype)
    if with_router:
        hl = (h - hb.astype(F32)).astype(BF16)
        logits = _dot(hb, rh_ref[...]) + (_dot(hb, rl_ref[...]) + _dot(hl, rh_ref[...]))
        lane = lax.broadcasted_iota(jnp.int32, logits.shape, 1).astype(F32)
        logits = jnp.where(lane < N_EXPERTS, logits, NEG)
        v1 = logits.max(axis=-1, keepdims=True)
        i1 = jnp.where(logits == v1, lane, float(LANES)).min(axis=-1, keepdims=True)
        rest = jnp.where(lane == i1, NEG, logits)
        v2 = rest.max(axis=-1, keepdims=True)
        i2 = jnp.where(rest == v2, lane, float(LANES)).min(axis=-1, keepdims=True)
        e2 = jnp.exp(v2 - v1)
        den = 1.0 + e2
        go_ref[...] = (jnp.where(lane == 0.0, i1, 0.0) + jnp.where(lane == 1.0, i2, 0.0)
                       + jnp.where(lane == 2.0, 1.0 / den, 0.0) + jnp.where(lane == 3.0, e2 / den, 0.0))


def _outproj_call(oa, ob, oc, w_out, x2d, gate, g2, shift, scale, seq, tm, router=None):
    t_rows, d = x2d.shape
    tpb = seq // tm
    wa, wb, wc = oa.shape[1], ob.shape[1], oc.shape[1]
    assert wa == wb and (wa + wb) % wc == 0
    row = lambda i: (i, 0)
    fixed = lambda i: (0, 0)
    per_b = lambda i: (i // tpb, 0, 0)
    in_specs = [pl.BlockSpec((tm, wa), row), pl.BlockSpec((tm, wb), row), pl.BlockSpec((tm, wc), row),
                pl.BlockSpec((wa, d), lambda i: (0, 0)), pl.BlockSpec((wb, d), lambda i: (1, 0)),
                pl.BlockSpec((wc, d), lambda i: ((wa + wb) // wc, 0)),
                pl.BlockSpec((tm, d), row), pl.BlockSpec((1, 1, d), per_b), pl.BlockSpec((1, d), fixed),
                pl.BlockSpec((1, 1, d), per_b), pl.BlockSpec((1, 1, d), per_b)]
    args = [oa, ob, oc, w_out, w_out, w_out, x2d, gate, g2, shift, scale]
    out_specs = [pl.BlockSpec((tm, d), row), pl.BlockSpec((tm, d), row)]
    out_shape = [jax.ShapeDtypeStruct((t_rows, d), F32),
                 jax.ShapeDtypeStruct((t_rows, d), BF16 if router is None else F32)]
    if router is not None:
        in_specs += [pl.BlockSpec((d, LANES), fixed), pl.BlockSpec((d, LANES), fixed)]
        args += list(router)
        out_specs.append(pl.BlockSpec((tm, LANES), row))
        out_shape.append(jax.ShapeDtypeStruct((t_rows, LANES), F32))
    return pl.pallas_call(
        functools.partial(_outproj_kernel, with_router=router is not None),
        grid=(t_rows // tm,),
        in_specs=in_specs, out_specs=out_specs, out_shape=out_shape,
        compiler_params=_cparams(("arbitrary",)),
        name="outproj",
    )(*args)


def _swiglu_step(h, w1_ref, w3_ref, w2_ref, acc_scr):
    a = _dot(h, w1_ref[...])
    b = _dot(h, w3_ref[...])
    act = (a * jax.nn.sigmoid(a) * b).astype(BF16)
    acc_scr[...] += _dot(act, w2_ref[...])


def _ffn_kernel(h_ref, w1_ref, w3_ref, w2_ref, x_ref, gate_ref, o_ref, acc_scr):
    f = pl.program_id(1)

    @pl.when(f == 0)
    def _():
        acc_scr[...] = jnp.zeros(acc_scr.shape, F32)

    _swiglu_step(h_ref[...], w1_ref, w3_ref, w2_ref, acc_scr)

    @pl.when(f == pl.num_programs(1) - 1)
    def _():
        o_ref[...] = x_ref[...] + gate_ref[0] * acc_scr[...]


def _ffn_call(h, w1, w3, w2, x2d, gate, seq, tm, tf=512):
    t_rows, d = x2d.shape
    ff = w1.shape[1]
    tpb = seq // tm
    return pl.pallas_call(
        _ffn_kernel,
        grid=(t_rows // tm, ff // tf),
        in_specs=[pl.BlockSpec((tm, d), lambda i, f: (i, 0)),
                  pl.BlockSpec((d, tf), lambda i, f: (0, f)),
                  pl.BlockSpec((d, tf), lambda i, f: (0, f)),
                  pl.BlockSpec((tf, d), lambda i, f: (f, 0)),
                  pl.BlockSpec((tm, d), lambda i, f: (i, 0)),
                  pl.BlockSpec((1, 1, d), lambda i, f: (i // tpb, 0, 0))],
        out_specs=pl.BlockSpec((tm, d), lambda i, f: (i, 0)),
        out_shape=jax.ShapeDtypeStruct((t_rows, d), F32),
        scratch_shapes=[pltpu.VMEM((tm, d), F32)],
        compiler_params=_cparams(("arbitrary", "arbitrary")),
        name="ffn",
    )(h, w1, w3, w2, x2d, gate)


MOE_TM = 512
ROW_UNROLL = 8


def _route_plan(info, tm):
    t_rows = info.shape[0]
    e_flat = info[:, :2].astype(jnp.int32).T.reshape(-1)
    onehot = (e_flat[:, None] == jnp.arange(N_EXPERTS, dtype=jnp.int32)[None, :]).astype(jnp.int32)
    csum = jnp.cumsum(onehot, axis=0)
    counts = csum[-1]
    padded = ((counts + tm - 1) // tm) * tm
    pend = jnp.cumsum(padded)
    pstart = pend - padded
    dest = jnp.sum(onehot * (csum - 1 + pstart[None, :]), axis=1).astype(jnp.int32)
    n_tiles = (2 * t_rows) // tm + N_EXPERTS
    n_used = (pend[-1] // tm).astype(jnp.int32)
    tile_idx = jnp.arange(n_tiles, dtype=jnp.int32)
    owner = jnp.sum((tile_idx[:, None] * tm >= pend[None, :]).astype(jnp.int32), axis=1)
    last_owner = jnp.sum(((n_used - 1) * tm >= pend).astype(jnp.int32))
    tile_expert = jnp.where(tile_idx < n_used, owner, last_owner).astype(jnp.int32)
    return dest, tile_expert, n_used.reshape(1)


def _row_copy_loop(n_rows, start_one):
    def issue(r, carry):
        for cp in start_one(r):
            cp.start()
        return carry
    lax.fori_loop(0, n_rows, issue, 0, unroll=ROW_UNROLL)

    def drain(r, carry):
        for cp in start_one(r):
            cp.wait()
        return carry
    lax.fori_loop(0, n_rows, drain, 0, unroll=ROW_UNROLL)


def _scatter_kernel(dest_ref, h_ref, xg_in, xg_out, sem, *, ts, t_rows):
    del xg_in
    base = pl.program_id(0) * ts

    def copies(r):
        t = base + r
        return [pltpu.make_async_copy(h_ref.at[pl.ds(r, 1)], xg_out.at[pl.ds(dest_ref[k * t_rows + t], 1)], sem)
                for k in range(2)]
    _row_copy_loop(ts, copies)


def _scatter_call(dest, h, n_rows, ts=256):
    t_rows, d = h.shape
    grid_spec = pltpu.PrefetchScalarGridSpec(
        num_scalar_prefetch=1, grid=(t_rows // ts,),
        in_specs=[pl.BlockSpec((ts, d), lambda i, dst: (i, 0)), pl.BlockSpec(memory_space=pl.ANY)],
        out_specs=pl.BlockSpec(memory_space=pl.ANY),
        scratch_shapes=[pltpu.SemaphoreType.DMA(())])
    return pl.pallas_call(
        functools.partial(_scatter_kernel, ts=ts, t_rows=t_rows),
        grid_spec=grid_spec,
        out_shape=jax.ShapeDtypeStruct((n_rows, d), h.dtype),
        input_output_aliases={2: 0},
        compiler_params=_cparams(("arbitrary",)),
        name="moe_scatter",
    )(dest, h, jnp.zeros((n_rows, d), h.dtype))


def _moe_kernel(te_ref, nu_ref, xg_ref, w1_ref, w3_ref, w2_ref, y_ref, hb_scr, acc_scr):
    del te_ref
    i = pl.program_id(0)
    f = pl.program_id(1)

    @pl.when(i < nu_ref[0])
    def _():
        @pl.when(f == 0)
        def _():
            hb_scr[...] = xg_ref[...].astype(BF16)
            acc_scr[...] = jnp.zeros(acc_scr.shape, F32)

        _swiglu_step(hb_scr[...], w1_ref.at[0], w3_ref.at[0], w2_ref.at[0], acc_scr)

        @pl.when(f == pl.num_programs(1) - 1)
        def _():
            y_ref[...] = acc_scr[...]

    @pl.when((i >= nu_ref[0]) & (f == 0))
    def _():
        y_ref[...] = jnp.zeros(y_ref.shape, F32)


def _moe_call(tile_expert, n_used, xg, w1, w3, w2, tm, tf=512):
    n_rows, d = xg.shape
    ff = w1.shape[2]
    nf = ff // tf
    f_eff = lambda i, f, nu: jnp.where(i < nu[0], f, nf - 1)
    grid_spec = pltpu.PrefetchScalarGridSpec(
        num_scalar_prefetch=2, grid=(n_rows // tm, nf),
        in_specs=[pl.BlockSpec((tm, d), lambda i, f, te, nu: (i, 0)),
                  pl.BlockSpec((1, d, tf), lambda i, f, te, nu: (te[i], 0, f_eff(i, f, nu))),
                  pl.BlockSpec((1, d, tf), lambda i, f, te, nu: (te[i], 0, f_eff(i, f, nu))),
                  pl.BlockSpec((1, tf, d), lambda i, f, te, nu: (te[i], f_eff(i, f, nu), 0))],
        out_specs=pl.BlockSpec((tm, d), lambda i, f, te, nu: (i, 0)),
        scratch_shapes=[pltpu.VMEM((tm, d), BF16), pltpu.VMEM((tm, d), F32)])
    return pl.pallas_call(
        _moe_kernel,
        grid_spec=grid_spec,
        out_shape=jax.ShapeDtypeStruct((n_rows, d), F32),
        compiler_params=_cparams(("arbitrary", "arbitrary")),
        name="moe_ffn",
    )(tile_expert, n_used, xg, w1, w3, w2)


def _combine_kernel(dest_ref, info_ref, x_ref, gate_ref, y_hbm, o_ref, buf, sem, *, tc, t_rows):
    base = pl.program_id(0) * tc

    def copies(r):
        t = base + r
        return [pltpu.make_async_copy(y_hbm.at[pl.ds(dest_ref[k * t_rows + t], 1)], buf.at[k, pl.ds(r, 1)], sem)
                for k in range(2)]
    _row_copy_loop(tc, copies)
    info = info_ref[...]
    y = info[:, 2:3] * buf[0] + info[:, 3:4] * buf[1]
    o_ref[...] = x_ref[...] + gate_ref[0] * y


def _combine_call(dest, info, x2d, gate, y, seq, tc=256):
    t_rows, d = x2d.shape
    tpb = seq // tc
    grid_spec = pltpu.PrefetchScalarGridSpec(
        num_scalar_prefetch=1, grid=(t_rows // tc,),
        in_specs=[pl.BlockSpec((tc, LANES), lambda i, dst: (i, 0)),
                  pl.BlockSpec((tc, d), lambda i, dst: (i, 0)),
                  pl.BlockSpec((1, 1, d), lambda i, dst: (i // tpb, 0, 0)),
                  pl.BlockSpec(memory_space=pl.ANY)],
        out_specs=pl.BlockSpec((tc, d), lambda i, dst: (i, 0)),
        scratch_shapes=[pltpu.VMEM((2, tc, d), F32), pltpu.SemaphoreType.DMA(())])
    return pl.pallas_call(
        functools.partial(_combine_kernel, tc=tc, t_rows=t_rows),
        grid_spec=grid_spec,
        out_shape=jax.ShapeDtypeStruct((t_rows, d), F32),
        compiler_params=_cparams(("arbitrary",)),
        name="moe_combine",
    )(dest, info, x2d, gate, y)


def _moe_block(h_f32, info, x2d, gate, w1, w3, w2, seq):
    t_rows = x2d.shape[0]
    dest, tile_expert, n_used = _route_plan(info, MOE_TM)
    n_rows = 2 * t_rows + N_EXPERTS * MOE_TM
    xg = _scatter_call(dest, h_f32, n_rows)
    y = _moe_call(tile_expert, n_used, xg, w1, w3, w2, MOE_TM)
    return _combine_call(dest, info, x2d, gate, y, seq)


def _rope_table(n_tokens, head_dim):
    axis_dim = head_dim // 2
    inv_freq = ROPE_THETA ** (-jnp.arange(0, axis_dim, 2, dtype=F32) / axis_dim)
    t = jnp.arange(n_tokens, dtype=jnp.int32)
    row = (t // GRID_W).astype(F32)
    col = (t % GRID_W).astype(F32)
    ang_r = row[:, None] * inv_freq[None, :]
    ang_c = col[:, None] * inv_freq[None, :]
    cos = jnp.concatenate([jnp.cos(ang_r)] * 2 + [jnp.cos(ang_c)] * 2, axis=-1)
    sin = jnp.concatenate([-jnp.sin(ang_r), jnp.sin(ang_r), -jnp.sin(ang_c), jnp.sin(ang_c)], axis=-1)
    reps = LANES // head_dim
    return jnp.tile(cos, (1, reps)), jnp.tile(sin, (1, reps))


def _identity_rope_table(n_tokens):
    return jnp.ones((n_tokens, LANES), F32), jnp.zeros((n_tokens, LANES), F32)


def _na_bias_selectors(seq):
    rows = seq // GRID_W
    nblk = rows // NA_QROWS
    col = np.arange(GRID_W)
    c0 = np.clip(col - NA_WIN_C // 2, 0, GRID_W - NA_WIN_C)[:, None]
    col_ok = (col[None, :] >= c0) & (col[None, :] < c0 + NA_WIN_C)
    ic = col[None, :] - col[:, None] + NA_WIN_C - 1
    sel_c = (col_ok[..., None] & (ic[..., None] == np.arange(2 * NA_WIN_C - 1))).astype(np.float32)
    sel_r = np.zeros((3, NA_QROWS, NA_KROWS, 2 * NA_WIN_R - 1), np.float32)
    row_ok = np.zeros((3, NA_QROWS, NA_KROWS), bool)
    for v, blk in enumerate((0, 1, nblk - 1)):
        r_base = blk * NA_QROWS
        u0 = int(np.clip(r_base - NA_WIN_R // 2, 0, rows - NA_KROWS))
        for q in range(NA_QROWS):
            r = r_base + q
            r0 = int(np.clip(r - NA_WIN_R // 2, 0, rows - NA_WIN_R))
            for k in range(NA_KROWS):
                kr = u0 + k
                if r0 <= kr < r0 + NA_WIN_R:
                    row_ok[v, q, k] = True
                    sel_r[v, q, k, kr - r + NA_WIN_R - 1] = 1.0
    valid = row_ok[:, :, None, :, None] & col_ok[None, None, :, None, :]
    return sel_r, sel_c, valid.reshape(3, NA_TQ, NA_TK)


def _na_bias_table(rpb, seq):
    sel_r, sel_c, valid = _na_bias_selectors(seq)
    hp = lax.Precision.HIGHEST
    t_col = jnp.einsum('hdi,cxi->hdcx', rpb.astype(F32), sel_c, precision=hp)
    full = jnp.einsum('vqkd,hdcx->hvqckx', sel_r, t_col, precision=hp)
    full = full.reshape(rpb.shape[0], 3, NA_TQ, NA_TK)
    return jnp.where(valid[None], full, NEG)


def _sw_mask_table(seq):
    nblk = seq // SW_TQ
    out = []
    for blk in (0, 1, nblk - 1):
        start = int(np.clip(blk * SW_TQ - SW_WINDOW, 0, seq - SW_TK))
        qpos = (blk * SW_TQ + np.arange(SW_TQ))[:, None]
        kpos = (start + np.arange(SW_TK))[None, :]
        out.append(np.where(np.abs(qpos - kpos) <= SW_WINDOW, 0.0, NEG))
    return jnp.asarray(np.stack(out), F32)


def _col_gains(na_q, na_k, sw_q, sw_k, df_q, df_k):
    one = lambda n: jnp.ones((n,), F32)
    df_scale = DF_QK_DIM ** -0.5
    return jnp.concatenate([
        jnp.tile(na_q, NA_HEADS), jnp.tile(na_k, NA_HEADS), one(NA_HEADS * HEAD_DIM),
        jnp.tile(sw_q, SW_HEADS), jnp.tile(sw_k, SW_KV_HEADS), one(SW_KV_HEADS * HEAD_DIM),
        jnp.tile(df_q * df_scale, 2 * DF_HEADS), jnp.tile(df_k, 2 * DF_HEADS), one(DF_HEADS * DF_V_DIM),
    ]).astype(F32)[None, :]


def kernel(x, c, ctx, c_ctx, ada_w, ada_b, norm1_g, norm2_g, w_in, w_out, na_q_norm, na_k_norm, na_rpb,
           sw_q_norm, sw_k_norm, sw_sink, df_q_norm, df_k_norm, df_lambda, df_subln_g,
           ffn_w1, ffn_w3, ffn_w2, moe_router, moe_w1, moe_w3, moe_w2):
    b, seq, d = x.shape
    lc = ctx.shape[1]
    depth = ada_w.shape[0]
    assert d == D_MODEL and seq % (NA_TILES * NA_TQ) == 0 and seq % (2 * DF_TK) == 0

    c_rows = jnp.concatenate([c, c_ctx[None], jnp.zeros((16 - b - 1, d), F32)], axis=0)
    mod = _ada_call(c_rows, ada_w, ada_b)

    tabs_x = _rope_table(seq, HEAD_DIM) + _rope_table(seq, DF_QK_DIM)
    tabs_c = _identity_rope_table(lc) * 2
    sw_mask = _sw_mask_table(seq)

    x2 = x.reshape(b * seq, d)
    xc2 = ctx.reshape(b * lc, d)
    tm_x = 512
    tm_c = lc

    for i in range(depth):
        last = i == depth - 1
        m = mod[i].reshape(16, 6, d)
        mx = [m[:b, j][:, None, :] for j in range(6)]
        mc = [jnp.broadcast_to(m[b:b + 1, j][:, None, :], (b, 1, d)) for j in range(6)]
        lam_init = 0.8 - 0.6 * math.exp(-0.3 * i)
        gains = _col_gains(na_q_norm[i], na_k_norm[i], sw_q_norm[i], sw_k_norm[i], df_q_norm[i], df_k_norm[i])
        g1 = norm1_g[i][None, :]
        g2 = norm2_g[i][None, :]
        w_in_b = w_in[i].astype(BF16)
        w_out_b = w_out[i].astype(BF16)
        sink = sw_sink[i].astype(F32)
        lam_p = df_lambda[i][None].astype(F32)
        subln = df_subln_g[i][None, :]

        p = _proj_call(x2, mx[0], mx[1], g1, w_in_b, gains, tabs_x, seq, tm_x).reshape(b, seq, IN_WIDTH)
        pc = _proj_call(xc2, mc[0], mc[1], g1, w_in_b, gains, tabs_c, lc, tm_c).reshape(b, lc, IN_WIDTH)
        oa = _na_call(p, pc, _na_bias_table(na_rpb[i], seq))
        ob = _sw_call(p, pc, sink, sw_mask)
        oc = _df_call(p, p, pc, lam_p, subln, lam_init, True)

        j = i // 2
        moe = i % 2 == 1
        if moe:
            r = jnp.pad(moe_router[j].astype(F32), ((0, 0), (0, LANES - N_EXPERTS)))
            r_hi = r.astype(BF16)
            r_lo = (r - r_hi.astype(F32)).astype(BF16)
            router = (r_hi, r_lo)
            w1, w3, w2 = moe_w1[j].astype(BF16), moe_w3[j].astype(BF16), moe_w2[j].astype(BF16)
        else:
            router = None
            w1, w3, w2 = ffn_w1[j].astype(BF16), ffn_w3[j].astype(BF16), ffn_w2[j].astype(BF16)

        def mix_and_ffn(oa_, ob_, oc_, xs, mv, seq_, tm_):
            flat = lambda o: o.reshape(xs.shape[0], o.shape[-1])
            res = _outproj_call(flat(oa_), flat(ob_), flat(oc_), w_out_b, xs, mv[2], g2, mv[3], mv[4],
                                seq_, min(tm_, 256), router)
            if moe:
                return _moe_block(res[1], res[2], res[0], mv[5], w1, w3, w2, seq_)
            return _ffn_call(res[1], w1, w3, w2, res[0], mv[5], seq_, tm_)

        if not last:
            coa = _ctx_call(pc, sink, COL_QA, COL_KA, COL_VA, NA_HEADS, 1, False)
            cob = _ctx_call(pc, sink, COL_QB, COL_KB, COL_VB, SW_KV_HEADS, SW_GROUP, True)
            coc = _df_call(pc, None, pc, lam_p, subln, lam_init, False)
            xc2 = mix_and_ffn(coa, cob, coc, xc2, mc, lc, tm_c)
        x2 = mix_and_ffn(oa, ob, oc, x2, mx, seq, tm_x)
    return x2.reshape(b, seq, d)
```

```python
import functools
import math

import numpy as np
import jax
import jax.numpy as jnp
from jax import lax
from jax.experimental import pallas as pl
from jax.experimental.pallas import tpu as pltpu

F32 = jnp.float32
BF16 = jnp.bfloat16

D_MODEL = 2048
GRID_W = 64
HEAD_DIM = 128
NA_HEADS = 6
NA_WIN_R = 8
NA_WIN_C = 16
SW_HEADS = 6
SW_KV_HEADS = 2
SW_GROUP = SW_HEADS // SW_KV_HEADS
SW_WINDOW = 128
DF_HEADS = 4
DF_QK_DIM = 64
DF_V_DIM = 128
IN_WIDTH = 5120
D_FF = 5632
N_EXPERTS = 8
ROPE_THETA = 10000.0
NORM_EPS = 1e-6
NEG = -1e30

LANES = 128
VMEM_LIMIT = 56 * 1024 * 1024

COL_QA, COL_KA, COL_VA = 0, 6, 12
COL_QB, COL_KB, COL_VB = 18, 24, 26
COL_QC, COL_KC, COL_VC = 28, 32, 36

PROJ_CHUNK = 256
T_PLAIN, T_NORM, T_NORM_ROPE_B, T_NORM_ROPE_C = 0, 1, 2, 3
CHUNK_TYPES = ((T_NORM,) * 6 + (T_PLAIN,) * 3 + (T_NORM_ROPE_B,) * 4 + (T_PLAIN,)
               + (T_NORM_ROPE_C,) * 4 + (T_PLAIN,) * 2)

NA_QROWS = 4
NA_KROWS = 12
NA_TQ = NA_QROWS * GRID_W
NA_TK = NA_KROWS * GRID_W
NA_TILES = 4
SW_TQ = 256
SW_TK = SW_TQ + 2 * SW_WINDOW
DF_TQ = 256
DF_TK = 512
DF_TILES = 2


def _cparams(sem):
    return pltpu.CompilerParams(dimension_semantics=sem, vmem_limit_bytes=VMEM_LIMIT)


def _dot(a, b):
    return jnp.dot(a, b, preferred_element_type=F32)


def _dot_nt(a, b):
    return lax.dot_general(a, b, (((1,), (1,)), ((), ())), preferred_element_type=F32)


def _ada_kernel(c_ref, w_ref, b_ref, o_ref):
    c = c_ref[...]
    sc = (c * jax.nn.sigmoid(c)).astype(BF16)
    o_ref[0] = _dot(sc, w_ref[0].astype(BF16)) + b_ref[0]


def _ada_call(c_rows, ada_w, ada_b):
    depth, d, n = ada_w.shape
    rows = c_rows.shape[0]
    tn = 1024
    return pl.pallas_call(
        _ada_kernel,
        grid=(depth, n // tn),
        in_specs=[pl.BlockSpec((rows, d), lambda l, j: (0, 0)),
                  pl.BlockSpec((1, d, tn), lambda l, j: (l, 0, j)),
                  pl.BlockSpec((1, 1, tn), lambda l, j: (l, 0, j))],
        out_specs=pl.BlockSpec((1, rows, tn), lambda l, j: (l, 0, j)),
        out_shape=jax.ShapeDtypeStruct((depth, rows, n), F32),
        compiler_params=_cparams(("arbitrary", "arbitrary")),
        name="ada",
    )(c_rows, ada_w, ada_b.reshape(depth, 1, n))


def _rms(v, gain):
    ms = jnp.mean(v * v, axis=-1, keepdims=True)
    return v * lax.rsqrt(ms + NORM_EPS) * gain


def _rms_half(v, gain, lane):
    lo = lane < DF_QK_DIM
    sq = v * v
    s_lo = jnp.sum(jnp.where(lo, sq, 0.0), axis=-1, keepdims=True)
    s_hi = jnp.sum(jnp.where(lo, 0.0, sq), axis=-1, keepdims=True)
    ms = jnp.where(lo, s_lo, s_hi) * (1.0 / DF_QK_DIM)
    return v * lax.rsqrt(ms + NORM_EPS) * gain


def _rope(v, cos, sin, half, lane):
    fwd = pltpu.roll(v, LANES - half, axis=1)
    bwd = pltpu.roll(v, half, axis=1)
    partner = jnp.where((lane & half) == 0, fwd, bwd)
    return v * cos + partner * sin


def _proj_kernel(x_ref, shift_ref, scale_ref, g_ref, w_ref, gain_ref,
                 cb_ref, sb_ref, cc_ref, sc_ref, o_ref):
    h = (_rms(x_ref[...], g_ref[...]) * (1.0 + scale_ref[0]) + shift_ref[0]).astype(BF16)
    lane = lax.broadcasted_iota(jnp.int32, (h.shape[0], LANES), 1)
    for c, t in enumerate(CHUNK_TYPES):
        p = _dot(h, w_ref[:, c * PROJ_CHUNK:(c + 1) * PROJ_CHUNK])
        for hh in range(PROJ_CHUNK // LANES):
            c0 = c * PROJ_CHUNK + hh * LANES
            v = p[:, hh * LANES:(hh + 1) * LANES]
            gain = gain_ref[:, c0:c0 + LANES]
            if t == T_NORM:
                v = _rms(v, gain)
            elif t == T_NORM_ROPE_B:
                v = _rope(_rms(v, gain), cb_ref[...], sb_ref[...], HEAD_DIM // 4, lane)
            elif t == T_NORM_ROPE_C:
                v = _rope(_rms_half(v, gain, lane), cc_ref[...], sc_ref[...], DF_QK_DIM // 4, lane)
            o_ref[:, c0:c0 + LANES] = v.astype(BF16)


def _proj_call(x2d, shift, scale, g, w, gains, tabs, seq, tm):
    t_rows, d = x2d.shape
    n = w.shape[1]
    assert n == PROJ_CHUNK * len(CHUNK_TYPES)
    tpb = seq // tm
    return pl.pallas_call(
        _proj_kernel,
        grid=(t_rows // tm,),
        in_specs=[pl.BlockSpec((tm, d), lambda i: (i, 0)),
                  pl.BlockSpec((1, 1, d), lambda i: (i // tpb, 0, 0)),
                  pl.BlockSpec((1, 1, d), lambda i: (i // tpb, 0, 0)),
                  pl.BlockSpec((1, d), lambda i: (0, 0)),
                  pl.BlockSpec((d, n), lambda i: (0, 0), pipeline_mode=pl.Buffered(1)),
                  pl.BlockSpec((1, n), lambda i: (0, 0))]
                 + [pl.BlockSpec((tm, LANES), lambda i: (i % tpb, 0))] * 4,
        out_specs=pl.BlockSpec((tm, n), lambda i: (i, 0)),
        out_shape=jax.ShapeDtypeStruct((t_rows, n), BF16),
        compiler_params=_cparams(("arbitrary",)),
        name="proj",
    )(x2d, shift, scale, g, w, gains, *tabs)


def _softmax_parts(parts):
    m = parts[0].max(axis=-1, keepdims=True)
    for s in parts[1:]:
        m = jnp.maximum(m, s.max(axis=-1, keepdims=True))
    return m


def _na_kernel(q_ref, k_ref, v_ref, kc_ref, vc_ref, bias_ref, o_ref, *, nblk):
    scale = HEAD_DIM ** -0.5

    def one_block(t):
        blk = pl.program_id(2) * NA_TILES + t
        u0 = jnp.clip(NA_QROWS * blk - NA_WIN_R // 2, 0, GRID_W - NA_KROWS)
        start = pl.multiple_of(u0 * GRID_W, GRID_W)
        var = jnp.where(blk == 0, 0, jnp.where(blk == nblk - 1, 2, 1))
        q = q_ref[0, t * NA_TQ:(t + 1) * NA_TQ, :]
        kw = k_ref[0, pl.ds(start, NA_TK), :]
        vw = v_ref[0, pl.ds(start, NA_TK), :]
        s_nb = _dot_nt(q, kw) * scale + bias_ref[0, var]
        s_cx = _dot_nt(q, kc_ref[0]) * scale
        m = jnp.maximum(s_nb.max(axis=-1, keepdims=True), s_cx.max(axis=-1, keepdims=True))
        p_nb = jnp.exp(s_nb - m)
        p_cx = jnp.exp(s_cx - m)
        l = p_nb.sum(axis=-1, keepdims=True) + p_cx.sum(axis=-1, keepdims=True)
        return (_dot(p_nb.astype(BF16), vw) + _dot(p_cx.astype(BF16), vc_ref[0])) / l

    o_ref[0] = jnp.concatenate([one_block(t) for t in range(NA_TILES)], axis=0).astype(BF16)


def _na_call(p, pc, bias):
    b, seq, _ = p.shape
    lc = pc.shape[1]
    nblk = seq // NA_TQ
    bq = NA_TILES * NA_TQ
    return pl.pallas_call(
        functools.partial(_na_kernel, nblk=nblk),
        grid=(b, NA_HEADS, seq // bq),
        in_specs=[pl.BlockSpec((1, bq, LANES), lambda bi, h, i: (bi, i, COL_QA + h)),
                  pl.BlockSpec((1, seq, LANES), lambda bi, h, i: (bi, 0, COL_KA + h)),
                  pl.BlockSpec((1, seq, LANES), lambda bi, h, i: (bi, 0, COL_VA + h)),
                  pl.BlockSpec((1, lc, LANES), lambda bi, h, i: (bi, 0, COL_KA + h)),
                  pl.BlockSpec((1, lc, LANES), lambda bi, h, i: (bi, 0, COL_VA + h)),
                  pl.BlockSpec((1, 3, NA_TQ, NA_TK), lambda bi, h, i: (h, 0, 0, 0))],
        out_specs=pl.BlockSpec((1, bq, LANES), lambda bi, h, i: (bi, i, h)),
        out_shape=jax.ShapeDtypeStruct((b, seq, NA_HEADS * HEAD_DIM), BF16),
        compiler_params=_cparams(("arbitrary", "arbitrary", "arbitrary")),
        name="na_attn",
    )(p, p, p, pc, pc, bias)


def _sw_kernel(sink_ref, q_ref, k_ref, v_ref, kc_ref, vc_ref, mask_ref, o_ref, *, nblk, seq):
    hkv = pl.program_id(1)
    blk = pl.program_id(2)
    start = pl.multiple_of(jnp.clip(blk * SW_TQ - SW_WINDOW, 0, seq - SW_TK), SW_WINDOW)
    var = jnp.where(blk == 0, 0, jnp.where(blk == nblk - 1, 2, 1))
    scale = HEAD_DIM ** -0.5
    kw = k_ref[0, pl.ds(start, SW_TK), :]
    vw = v_ref[0, pl.ds(start, SW_TK), :]
    kc = kc_ref[0]
    vc = vc_ref[0]
    mask = mask_ref[var]
    outs = []
    for g in range(SW_GROUP):
        q = q_ref[0, :, g * LANES:(g + 1) * LANES]
        sink = sink_ref[hkv * SW_GROUP + g]
        s_w = _dot_nt(q, kw) * scale + mask
        s_c = _dot_nt(q, kc) * scale
        m = jnp.maximum(s_w.max(axis=-1, keepdims=True), s_c.max(axis=-1, keepdims=True))
        m = jnp.maximum(m, sink)
        p_w = jnp.exp(s_w - m)
        p_c = jnp.exp(s_c - m)
        l = p_w.sum(axis=-1, keepdims=True) + p_c.sum(axis=-1, keepdims=True) + jnp.exp(sink - m)
        o = _dot(p_w.astype(BF16), vw) + _dot(p_c.astype(BF16), vc)
        outs.append(o / l)
    o_ref[0] = jnp.concatenate(outs, axis=1).astype(BF16)


def _sw_call(p, pc, sink, mask):
    b, seq, _ = p.shape
    lc = pc.shape[1]
    nblk = seq // SW_TQ
    gw = SW_GROUP * LANES
    grid_spec = pltpu.PrefetchScalarGridSpec(
        num_scalar_prefetch=1,
        grid=(b, SW_KV_HEADS, nblk),
        in_specs=[pl.BlockSpec((1, SW_TQ, gw), lambda bi, h, i, s: (bi, i, COL_QB // SW_GROUP + h)),
                  pl.BlockSpec((1, seq, LANES), lambda bi, h, i, s: (bi, 0, COL_KB + h)),
                  pl.BlockSpec((1, seq, LANES), lambda bi, h, i, s: (bi, 0, COL_VB + h)),
                  pl.BlockSpec((1, lc, LANES), lambda bi, h, i, s: (bi, 0, COL_KB + h)),
                  pl.BlockSpec((1, lc, LANES), lambda bi, h, i, s: (bi, 0, COL_VB + h)),
                  pl.BlockSpec((3, SW_TQ, SW_TK), lambda bi, h, i, s: (0, 0, 0))],
        out_specs=pl.BlockSpec((1, SW_TQ, gw), lambda bi, h, i, s: (bi, i, h)))
    return pl.pallas_call(
        functools.partial(_sw_kernel, nblk=nblk, seq=seq),
        grid_spec=grid_spec,
        out_shape=jax.ShapeDtypeStruct((b, seq, SW_HEADS * HEAD_DIM), BF16),
        compiler_params=_cparams(("arbitrary", "arbitrary", "arbitrary")),
        name="sw_attn",
    )(sink, p, p, p, pc, pc, mask)


def _df_lambda(lam_ref, lam_init):
    lp = lam_ref[0]
    s1 = jnp.sum(jnp.sum(lp[0:1] * lp[1:2], axis=-1, keepdims=True), axis=0, keepdims=True)
    s2 = jnp.sum(jnp.sum(lp[2:3] * lp[3:4], axis=-1, keepdims=True), axis=0, keepdims=True)
    return jnp.exp(s1) - jnp.exp(s2) + lam_init


def _df_kernel(lam_ref, g_ref, q_ref, *refs, lam_init, with_latent, tq):
    if with_latent:
        k_ref, v_ref, kc_ref, vc_ref, o_ref = refs
    else:
        kc_ref, vc_ref, o_ref = refs
    chunks = []
    if with_latent:
        for c in range(k_ref.shape[1] // DF_TK):
            chunks.append((k_ref[0, c * DF_TK:(c + 1) * DF_TK, :], v_ref[0, c * DF_TK:(c + 1) * DF_TK, :]))
    chunks.append((kc_ref[0], vc_ref[0]))
    max_chunks = [kc_ref[0]]
    if with_latent:
        max_chunks += [k_ref[0, c * 2 * DF_TK:(c + 1) * 2 * DF_TK, :] for c in range(k_ref.shape[1] // (2 * DF_TK))]
    lam = _df_lambda(lam_ref, lam_init)

    def lane_fold(x, op, acc=None):
        for j in range(x.shape[1] // LANES):
            blk = x[:, j * LANES:(j + 1) * LANES]
            acc = blk if acc is None else op(acc, blk)
        return acc

    def one_tile(q):
        lane = lax.broadcasted_iota(jnp.int32, q.shape, 1)
        zero = jnp.zeros_like(q)
        q2 = jnp.concatenate([jnp.where(lane < DF_QK_DIM, q, zero),
                              jnp.where(lane < DF_QK_DIM, zero, q)], axis=0)
        m_part = None
        for kb in max_chunks:
            m_part = lane_fold(_dot_nt(q2, kb), jnp.maximum, m_part)
        m = m_part.max(axis=-1, keepdims=True)
        l_part = None
        acc = jnp.zeros((2 * tq, DF_V_DIM), F32)
        for kb, vb in chunks:
            pr = jnp.exp(_dot_nt(q2, kb) - m)
            l_part = lane_fold(pr, jnp.add, l_part)
            acc = acc + _dot(pr.astype(BF16), vb)
        o2 = acc / l_part.sum(axis=-1, keepdims=True)
        o = o2[:tq] - lam * o2[tq:]
        return _rms(o, g_ref[...]) * (1.0 - lam_init)

    outs = [one_tile(q_ref[0, t * tq:(t + 1) * tq, :]) for t in range(q_ref.shape[1] // tq)]
    o_ref[0] = jnp.concatenate(outs, axis=0).astype(BF16)


def _df_call(pq, pk, pc, lam_p, subln_g, lam_init, with_latent):
    b, lq, _ = pq.shape
    lc = pc.shape[1]
    tq = min(DF_TQ, lq)
    bq = min(DF_TILES * tq, lq)
    in_specs = [pl.BlockSpec((1, 4, DF_QK_DIM), lambda bi, h, i: (0, 0, 0)),
                pl.BlockSpec((1, DF_V_DIM), lambda bi, h, i: (0, 0)),
                pl.BlockSpec((1, bq, LANES), lambda bi, h, i: (bi, i, COL_QC + h))]
    args = [lam_p, subln_g, pq]
    if with_latent:
        seq = pk.shape[1]
        in_specs += [pl.BlockSpec((1, seq, LANES), lambda bi, h, i: (bi, 0, COL_KC + h)),
                     pl.BlockSpec((1, seq, LANES), lambda bi, h, i: (bi, 0, COL_VC + h))]
        args += [pk, pk]
    in_specs += [pl.BlockSpec((1, lc, LANES), lambda bi, h, i: (bi, 0, COL_KC + h)),
                 pl.BlockSpec((1, lc, LANES), lambda bi, h, i: (bi, 0, COL_VC + h))]
    args += [pc, pc]
    return pl.pallas_call(
        functools.partial(_df_kernel, lam_init=lam_init, with_latent=with_latent, tq=tq),
        grid=(b, DF_HEADS, lq // bq),
        in_specs=in_specs,
        out_specs=pl.BlockSpec((1, bq, LANES), lambda bi, h, i: (bi, i, h)),
        out_shape=jax.ShapeDtypeStruct((b, lq, DF_HEADS * DF_V_DIM), BF16),
        compiler_params=_cparams(("arbitrary", "arbitrary", "arbitrary")),
        name="df_attn" if with_latent else "df_attn_ctx",
    )(*args)


def _ctx_kernel(sink_ref, q_ref, k_ref, v_ref, o_ref, *, group, has_sink):
    hkv = pl.program_id(1)
    scale = HEAD_DIM ** -0.5
    k = k_ref[0]
    v = v_ref[0]
    for g in range(group):
        q = q_ref[0, :, g * LANES:(g + 1) * LANES]
        s = _dot_nt(q, k) * scale
        m = s.max(axis=-1, keepdims=True)
        if has_sink:
            sink = sink_ref[hkv * group + g]
            m = jnp.maximum(m, sink)
        pr = jnp.exp(s - m)
        l = pr.sum(axis=-1, keepdims=True)
        if has_sink:
            l = l + jnp.exp(sink - m)
        o_ref[0, :, g * LANES:(g + 1) * LANES] = (_dot(pr.astype(BF16), v) / l).astype(BF16)


def _ctx_call(pc, sink, col_q, col_k, col_v, n_kv, group, has_sink):
    b, lc, _ = pc.shape
    gw = group * LANES
    grid_spec = pltpu.PrefetchScalarGridSpec(
        num_scalar_prefetch=1,
        grid=(b, n_kv),
        in_specs=[pl.BlockSpec((1, lc, gw), lambda bi, h, s: (bi, 0, col_q // group + h)),
                  pl.BlockSpec((1, lc, LANES), lambda bi, h, s: (bi, 0, col_k + h)),
                  pl.BlockSpec((1, lc, LANES), lambda bi, h, s: (bi, 0, col_v + h))],
        out_specs=pl.BlockSpec((1, lc, gw), lambda bi, h, s: (bi, 0, h)))
    return pl.pallas_call(
        functools.partial(_ctx_kernel, group=group, has_sink=has_sink),
        grid_spec=grid_spec,
        out_shape=jax.ShapeDtypeStruct((b, lc, n_kv * gw), BF16),
        compiler_params=_cparams(("arbitrary", "arbitrary")),
        name="ctx_attn",
    )(sink, pc, pc, pc)


def _outproj_kernel(oa_ref, ob_ref, oc_ref, wa_ref, wb_ref, wc_ref, x_ref, gate_ref, g_ref,
                    shift_ref, scale_ref, *refs, with_router):
    if with_router:
        rh_ref, rl_ref, xo_ref, ho_ref, go_ref = refs
    else:
        xo_ref, ho_ref = refs
    y = _dot(oa_ref[...], wa_ref[...]) + _dot(ob_ref[...], wb_ref[...]) + _dot(oc_ref[...], wc_ref[...])
    xn = x_ref[...] + gate_ref[0] * y
    xo_ref[...] = xn
    h = _rms(xn, g_ref[...]) * (1.0 + scale_ref[0]) + shift_ref[0]
    hb = h.astype(BF16)
    ho_ref[...] = h.astype(ho_ref.dtype)
    if with_router:
        hl = (h - hb.astype(F32)).astype(BF16)
        logits = _dot(hb, rh_ref[...]) + (_dot(hb, rl_ref[...]) + _dot(hl, rh_ref[...]))
        lane = lax.broadcasted_iota(jnp.int32, logits.shape, 1).astype(F32)
        logits = jnp.where(lane < N_EXPERTS, logits, NEG)
        v1 = logits.max(axis=-1, keepdims=True)
        i1 = jnp.where(logits == v1, lane, float(LANES)).min(axis=-1, keepdims=True)
        rest = jnp.where(lane == i1, NEG, logits)
        v2 = rest.max(axis=-1, keepdims=True)
        i2 = jnp.where(rest == v2, lane, float(LANES)).min(axis=-1, keepdims=True)
        e2 = jnp.exp(v2 - v1)
        den = 1.0 + e2
        go_ref[...] = (jnp.where(lane == 0.0, i1, 0.0) + jnp.where(lane == 1.0, i2, 0.0)
                       + jnp.where(lane == 2.0, 1.0 / den, 0.0) + jnp.where(lane == 3.0, e2 / den, 0.0))


def _outproj_call(oa, ob, oc, w_out, x2d, gate, g2, shift, scale, seq, tm, router=None):
    t_rows, d = x2d.shape
    tpb = seq // tm
    wa, wb, wc = oa.shape[1], ob.shape[1], oc.shape[1]
    assert wa == wb and (wa + wb) % wc == 0
    row = lambda i: (i, 0)
    fixed = lambda i: (0, 0)
    per_b = lambda i: (i // tpb, 0, 0)
    in_specs = [pl.BlockSpec((tm, wa), row), pl.BlockSpec((tm, wb), row), pl.BlockSpec((tm, wc), row),
                pl.BlockSpec((wa, d), lambda i: (0, 0)), pl.BlockSpec((wb, d), lambda i: (1, 0)),
                pl.BlockSpec((wc, d), lambda i: ((wa + wb) // wc, 0)),
                pl.BlockSpec((tm, d), row), pl.BlockSpec((1, 1, d), per_b), pl.BlockSpec((1, d), fixed),
                pl.BlockSpec((1, 1, d), per_b), pl.BlockSpec((1, 1, d), per_b)]
    args = [oa, ob, oc, w_out, w_out, w_out, x2d, gate, g2, shift, scale]
    out_specs = [pl.BlockSpec((tm, d), row), pl.BlockSpec((tm, d), row)]
    out_shape = [jax.ShapeDtypeStruct((t_rows, d), F32),
                 jax.ShapeDtypeStruct((t_rows, d), BF16 if router is None else F32)]
    if router is not None:
        in_specs += [pl.BlockSpec((d, LANES), fixed), pl.BlockSpec((d, LANES), fixed)]
        args += list(router)
        out_specs.append(pl.BlockSpec((tm, LANES), row))
        out_shape.append(jax.ShapeDtypeStruct((t_rows, LANES), F32))
    return pl.pallas_call(
        functools.partial(_outproj_kernel, with_router=router is not None),
        grid=(t_rows // tm,),
        in_specs=in_specs, out_specs=out_specs, out_shape=out_shape,
        compiler_params=_cparams(("arbitrary",)),
        name="outproj",
    )(*args)


def _swiglu_step(h, w1_ref, w3_ref, w2_ref, acc_scr):
    a = _dot(h, w1_ref[...])
    b = _dot(h, w3_ref[...])
    act = (a * jax.nn.sigmoid(a) * b).astype(BF16)
    acc_scr[...] += _dot(act, w2_ref[...])


def _ffn_kernel(h_ref, w1_ref, w3_ref, w2_ref, x_ref, gate_ref, o_ref):
    f = pl.program_id(1)

    @pl.when(f == 0)
    def _():
        o_ref[...] = jnp.zeros(o_ref.shape, F32)

    _swiglu_step(h_ref[...], w1_ref, w3_ref, w2_ref, o_ref)

    @pl.when(f == pl.num_programs(1) - 1)
    def _():
        o_ref[...] = x_ref[...] + gate_ref[0] * o_ref[...]


def _ffn_call(h, w1, w3, w2, x2d, gate, seq, tm, tf=512):
    t_rows, d = x2d.shape
    ff = w1.shape[1]
    tpb = seq // tm
    return pl.pallas_call(
        _ffn_kernel,
        grid=(t_rows // tm, ff // tf),
        in_specs=[pl.BlockSpec((tm, d), lambda i, f: (i, 0)),
                  pl.BlockSpec((d, tf), lambda i, f: (0, f)),
                  pl.BlockSpec((d, tf), lambda i, f: (0, f)),
                  pl.BlockSpec((tf, d), lambda i, f: (f, 0)),
                  pl.BlockSpec((tm, d), lambda i, f: (i, 0), pipeline_mode=pl.Buffered(1)),
                  pl.BlockSpec((1, 1, d), lambda i, f: (i // tpb, 0, 0))],
        out_specs=pl.BlockSpec((tm, d), lambda i, f: (i, 0)),
        out_shape=jax.ShapeDtypeStruct((t_rows, d), F32),
        compiler_params=_cparams(("arbitrary", "arbitrary")),
        name="ffn",
    )(h, w1, w3, w2, x2d, gate)


MOE_TM = 512
ROW_UNROLL = 8


def _route_plan(info, tm):
    t_rows = info.shape[0]
    e_flat = info[:, :2].astype(jnp.int32).T.reshape(-1)
    onehot = (e_flat[:, None] == jnp.arange(N_EXPERTS, dtype=jnp.int32)[None, :]).astype(jnp.int32)
    csum = jnp.cumsum(onehot, axis=0)
    counts = csum[-1]
    padded = ((counts + tm - 1) // tm) * tm
    pend = jnp.cumsum(padded)
    pstart = pend - padded
    dest = jnp.sum(onehot * (csum - 1 + pstart[None, :]), axis=1).astype(jnp.int32)
    n_tiles = (2 * t_rows) // tm + N_EXPERTS
    n_used = (pend[-1] // tm).astype(jnp.int32)
    tile_idx = jnp.arange(n_tiles, dtype=jnp.int32)
    owner = jnp.sum((tile_idx[:, None] * tm >= pend[None, :]).astype(jnp.int32), axis=1)
    last_owner = jnp.sum(((n_used - 1) * tm >= pend).astype(jnp.int32))
    tile_expert = jnp.where(tile_idx < n_used, owner, last_owner).astype(jnp.int32)
    return dest, tile_expert, n_used.reshape(1)


def _row_copy_loop(n_rows, start_one):
    def issue(r, carry):
        for cp in start_one(r):
            cp.start()
        return carry
    lax.fori_loop(0, n_rows, issue, 0, unroll=ROW_UNROLL)

    def drain(r, carry):
        for cp in start_one(r):
            cp.wait()
        return carry
    lax.fori_loop(0, n_rows, drain, 0, unroll=ROW_UNROLL)


def _scatter_kernel(dest_ref, h_ref, xg_in, xg_out, sem, *, ts, t_rows):
    del xg_in
    base = pl.program_id(0) * ts

    def copies(r):
        t = base + r
        return [pltpu.make_async_copy(h_ref.at[pl.ds(r, 1)], xg_out.at[pl.ds(dest_ref[k * t_rows + t], 1)], sem)
                for k in range(2)]
    _row_copy_loop(ts, copies)


def _scatter_call(dest, h, n_rows, ts=256):
    t_rows, d = h.shape
    grid_spec = pltpu.PrefetchScalarGridSpec(
        num_scalar_prefetch=1, grid=(t_rows // ts,),
        in_specs=[pl.BlockSpec((ts, d), lambda i, dst: (i, 0)), pl.BlockSpec(memory_space=pl.ANY)],
        out_specs=pl.BlockSpec(memory_space=pl.ANY),
        scratch_shapes=[pltpu.SemaphoreType.DMA(())])
    return pl.pallas_call(
        functools.partial(_scatter_kernel, ts=ts, t_rows=t_rows),
        grid_spec=grid_spec,
        out_shape=jax.ShapeDtypeStruct((n_rows, d), h.dtype),
        input_output_aliases={2: 0},
        compiler_params=_cparams(("arbitrary",)),
        name="moe_scatter",
    )(dest, h, jnp.zeros((n_rows, d), h.dtype))


def _moe_kernel(te_ref, nu_ref, xg_ref, w1_ref, w3_ref, w2_ref, y_ref, hb_scr):
    del te_ref
    i = pl.program_id(0)
    f = pl.program_id(1)

    @pl.when(f == 0)
    def _():
        y_ref[...] = jnp.zeros(y_ref.shape, F32)

    @pl.when(i < nu_ref[0])
    def _():
        @pl.when(f == 0)
        def _():
            hb_scr[...] = xg_ref[...].astype(BF16)

        _swiglu_step(hb_scr[...], w1_ref.at[0], w3_ref.at[0], w2_ref.at[0], y_ref)


def _moe_call(tile_expert, n_used, xg, w1, w3, w2, tm, tf=512):
    n_rows, d = xg.shape
    ff = w1.shape[2]
    nf = ff // tf
    f_eff = lambda i, f, nu: jnp.where(i < nu[0], f, nf - 1)
    grid_spec = pltpu.PrefetchScalarGridSpec(
        num_scalar_prefetch=2, grid=(n_rows // tm, nf),
        in_specs=[pl.BlockSpec((tm, d), lambda i, f, te, nu: (i, 0)),
                  pl.BlockSpec((1, d, tf), lambda i, f, te, nu: (te[i], 0, f_eff(i, f, nu))),
                  pl.BlockSpec((1, d, tf), lambda i, f, te, nu: (te[i], 0, f_eff(i, f, nu))),
                  pl.BlockSpec((1, tf, d), lambda i, f, te, nu: (te[i], f_eff(i, f, nu), 0))],
        out_specs=pl.BlockSpec((tm, d), lambda i, f, te, nu: (i, 0)),
        scratch_shapes=[pltpu.VMEM((tm, d), BF16)])
    return pl.pallas_call(
        _moe_kernel,
        grid_spec=grid_spec,
        out_shape=jax.ShapeDtypeStruct((n_rows, d), F32),
        compiler_params=_cparams(("arbitrary", "arbitrary")),
        name="moe_ffn",
    )(tile_expert, n_used, xg, w1, w3, w2)


def _combine_kernel(dest_ref, info_ref, x_ref, gate_ref, y_hbm, o_ref, buf, sem, *, tc, t_rows):
    base = pl.program_id(0) * tc

    def copies(r):
        t = base + r
        return [pltpu.make_async_copy(y_hbm.at[pl.ds(dest_ref[k * t_rows + t], 1)], buf.at[k, pl.ds(r, 1)], sem)
                for k in range(2)]
    _row_copy_loop(tc, copies)
    info = info_ref[...]
    y = info[:, 2:3] * buf[0] + info[:, 3:4] * buf[1]
    o_ref[...] = x_ref[...] + gate_ref[0] * y


def _combine_call(dest, info, x2d, gate, y, seq, tc=256):
    t_rows, d = x2d.shape
    tpb = seq // tc
    grid_spec = pltpu.PrefetchScalarGridSpec(
        num_scalar_prefetch=1, grid=(t_rows // tc,),
        in_specs=[pl.BlockSpec((tc, LANES), lambda i, dst: (i, 0)),
                  pl.BlockSpec((tc, d), lambda i, dst: (i, 0)),
                  pl.BlockSpec((1, 1, d), lambda i, dst: (i // tpb, 0, 0)),
                  pl.BlockSpec(memory_space=pl.ANY)],
        out_specs=pl.BlockSpec((tc, d), lambda i, dst: (i, 0)),
        scratch_shapes=[pltpu.VMEM((2, tc, d), F32), pltpu.SemaphoreType.DMA(())])
    return pl.pallas_call(
        functools.partial(_combine_kernel, tc=tc, t_rows=t_rows),
        grid_spec=grid_spec,
        out_shape=jax.ShapeDtypeStruct((t_rows, d), F32),
        compiler_params=_cparams(("arbitrary",)),
        name="moe_combine",
    )(dest, info, x2d, gate, y)


def _moe_block(h_f32, info, x2d, gate, w1, w3, w2, seq):
    t_rows = x2d.shape[0]
    dest, tile_expert, n_used = _route_plan(info, MOE_TM)
    n_rows = 2 * t_rows + N_EXPERTS * MOE_TM
    xg = _scatter_call(dest, h_f32, n_rows)
    y = _moe_call(tile_expert, n_used, xg, w1, w3, w2, MOE_TM)
    return _combine_call(dest, info, x2d, gate, y, seq)


def _rope_table(n_tokens, head_dim):
    axis_dim = head_dim // 2
    inv_freq = ROPE_THETA ** (-jnp.arange(0, axis_dim, 2, dtype=F32) / axis_dim)
    t = jnp.arange(n_tokens, dtype=jnp.int32)
    row = (t // GRID_W).astype(F32)
    col = (t % GRID_W).astype(F32)
    ang_r = row[:, None] * inv_freq[None, :]
    ang_c = col[:, None] * inv_freq[None, :]
    cos = jnp.concatenate([jnp.cos(ang_r)] * 2 + [jnp.cos(ang_c)] * 2, axis=-1)
    sin = jnp.concatenate([-jnp.sin(ang_r), jnp.sin(ang_r), -jnp.sin(ang_c), jnp.sin(ang_c)], axis=-1)
    reps = LANES // head_dim
    return jnp.tile(cos, (1, reps)), jnp.tile(sin, (1, reps))


def _identity_rope_table(n_tokens):
    return jnp.ones((n_tokens, LANES), F32), jnp.zeros((n_tokens, LANES), F32)


def _na_bias_selectors(seq):
    rows = seq // GRID_W
    nblk = rows // NA_QROWS
    col = np.arange(GRID_W)
    c0 = np.clip(col - NA_WIN_C // 2, 0, GRID_W - NA_WIN_C)[:, None]
    col_ok = (col[None, :] >= c0) & (col[None, :] < c0 + NA_WIN_C)
    ic = col[None, :] - col[:, None] + NA_WIN_C - 1
    sel_c = (col_ok[..., None] & (ic[..., None] == np.arange(2 * NA_WIN_C - 1))).astype(np.float32)
    sel_r = np.zeros((3, NA_QROWS, NA_KROWS, 2 * NA_WIN_R - 1), np.float32)
    row_ok = np.zeros((3, NA_QROWS, NA_KROWS), bool)
    for v, blk in enumerate((0, 1, nblk - 1)):
        r_base = blk * NA_QROWS
        u0 = int(np.clip(r_base - NA_WIN_R // 2, 0, rows - NA_KROWS))
        for q in range(NA_QROWS):
            r = r_base + q
            r0 = int(np.clip(r - NA_WIN_R // 2, 0, rows - NA_WIN_R))
            for k in range(NA_KROWS):
                kr = u0 + k
                if r0 <= kr < r0 + NA_WIN_R:
                    row_ok[v, q, k] = True
                    sel_r[v, q, k, kr - r + NA_WIN_R - 1] = 1.0
    valid = row_ok[:, :, None, :, None] & col_ok[None, None, :, None, :]
    return sel_r, sel_c, valid.reshape(3, NA_TQ, NA_TK)


def _na_bias_table(rpb, seq):
    sel_r, sel_c, valid = _na_bias_selectors(seq)
    hp = lax.Precision.HIGHEST
    t_col = jnp.einsum('hdi,cxi->hdcx', rpb.astype(F32), sel_c, precision=hp)
    full = jnp.einsum('vqkd,hdcx->hvqckx', sel_r, t_col, precision=hp)
    full = full.reshape(rpb.shape[0], 3, NA_TQ, NA_TK)
    return jnp.where(valid[None], full, NEG)


def _sw_mask_table(seq):
    nblk = seq // SW_TQ
    out = []
    for blk in (0, 1, nblk - 1):
        start = int(np.clip(blk * SW_TQ - SW_WINDOW, 0, seq - SW_TK))
        qpos = (blk * SW_TQ + np.arange(SW_TQ))[:, None]
        kpos = (start + np.arange(SW_TK))[None, :]
        out.append(np.where(np.abs(qpos - kpos) <= SW_WINDOW, 0.0, NEG))
    return jnp.asarray(np.stack(out), F32)


def _col_gains(na_q, na_k, sw_q, sw_k, df_q, df_k):
    one = lambda n: jnp.ones((n,), F32)
    df_scale = DF_QK_DIM ** -0.5
    return jnp.concatenate([
        jnp.tile(na_q, NA_HEADS), jnp.tile(na_k, NA_HEADS), one(NA_HEADS * HEAD_DIM),
        jnp.tile(sw_q, SW_HEADS), jnp.tile(sw_k, SW_KV_HEADS), one(SW_KV_HEADS * HEAD_DIM),
        jnp.tile(df_q * df_scale, 2 * DF_HEADS), jnp.tile(df_k, 2 * DF_HEADS), one(DF_HEADS * DF_V_DIM),
    ]).astype(F32)[None, :]


def kernel(x, c, ctx, c_ctx, ada_w, ada_b, norm1_g, norm2_g, w_in, w_out, na_q_norm, na_k_norm, na_rpb,
           sw_q_norm, sw_k_norm, sw_sink, df_q_norm, df_k_norm, df_lambda, df_subln_g,
           ffn_w1, ffn_w3, ffn_w2, moe_router, moe_w1, moe_w3, moe_w2):
    b, seq, d = x.shape
    lc = ctx.shape[1]
    depth = ada_w.shape[0]
    assert d == D_MODEL and seq % (NA_TILES * NA_TQ) == 0 and seq % (2 * DF_TK) == 0

    c_rows = jnp.concatenate([c, c_ctx[None], jnp.zeros((16 - b - 1, d), F32)], axis=0)
    mod = _ada_call(c_rows, ada_w, ada_b)

    tabs_x = _rope_table(seq, HEAD_DIM) + _rope_table(seq, DF_QK_DIM)
    tabs_c = _identity_rope_table(lc) * 2
    sw_mask = _sw_mask_table(seq)

    x2 = x.reshape(b * seq, d)
    xc2 = ctx.reshape(b * lc, d)
    tm_x = 512
    tm_c = lc

    for i in range(depth):
        last = i == depth - 1
        m = mod[i].reshape(16, 6, d)
        mx = [m[:b, j][:, None, :] for j in range(6)]
        mc = [jnp.broadcast_to(m[b:b + 1, j][:, None, :], (b, 1, d)) for j in range(6)]
        lam_init = 0.8 - 0.6 * math.exp(-0.3 * i)
        gains = _col_gains(na_q_norm[i], na_k_norm[i], sw_q_norm[i], sw_k_norm[i], df_q_norm[i], df_k_norm[i])
        g1 = norm1_g[i][None, :]
        g2 = norm2_g[i][None, :]
        w_in_b = w_in[i].astype(BF16)
        w_out_b = w_out[i].astype(BF16)
        sink = sw_sink[i].astype(F32)
        lam_p = df_lambda[i][None].astype(F32)
        subln = df_subln_g[i][None, :]

        p = _proj_call(x2, mx[0], mx[1], g1, w_in_b, gains, tabs_x, seq, tm_x).reshape(b, seq, IN_WIDTH)
        pc = _proj_call(xc2, mc[0], mc[1], g1, w_in_b, gains, tabs_c, lc, tm_c).reshape(b, lc, IN_WIDTH)
        oa = _na_call(p, pc, _na_bias_table(na_rpb[i], seq))
        ob = _sw_call(p, pc, sink, sw_mask)
        oc = _df_call(p, p, pc, lam_p, subln, lam_init, True)

        j = i // 2
        moe = i % 2 == 1
        if moe:
            r = jnp.pad(moe_router[j].astype(F32), ((0, 0), (0, LANES - N_EXPERTS)))
            r_hi = r.astype(BF16)
            r_lo = (r - r_hi.astype(F32)).astype(BF16)
            router = (r_hi, r_lo)
            w1, w3, w2 = moe_w1[j].astype(BF16), moe_w3[j].astype(BF16), moe_w2[j].astype(BF16)
        else:
            router = None
            w1, w3, w2 = ffn_w1[j].astype(BF16), ffn_w3[j].astype(BF16), ffn_w2[j].astype(BF16)

        def mix_and_ffn(oa_, ob_, oc_, xs, mv, seq_, tm_):
            flat = lambda o: o.reshape(xs.shape[0], o.shape[-1])
            res = _outproj_call(flat(oa_), flat(ob_), flat(oc_), w_out_b, xs, mv[2], g2, mv[3], mv[4],
                                seq_, min(tm_, 256), router)
            if moe:
                return _moe_block(res[1], res[2], res[0], mv[5], w1, w3, w2, seq_)
            return _ffn_call(res[1], w1, w3, w2, res[0], mv[5], seq_, min(2 * tm_, seq_))

        if not last:
            coa = _ctx_call(pc, sink, COL_QA, COL_KA, COL_VA, NA_HEADS, 1, False)
            cob = _ctx_call(pc, sink, COL_QB, COL_KB, COL_VB, SW_KV_HEADS, SW_GROUP, True)
            coc = _df_call(pc, None, pc, lam_p, subln, lam_init, False)
            xc2 = mix_and_ffn(coa, cob, coc, xc2, mc, lc, tm_c)
        x2 = mix_and_ffn(oa, ob, oc, x2, mx, seq, tm_x)
    return x2.reshape(b, seq, d)
```

```python
import functools
import math

import numpy as np
import jax
import jax.numpy as jnp
from jax import lax
from jax.experimental import pallas as pl
from jax.experimental.pallas import tpu as pltpu

F32 = jnp.float32
BF16 = jnp.bfloat16

D_MODEL = 2048
GRID_W = 64
HEAD_DIM = 128
NA_HEADS = 6
NA_WIN_R = 8
NA_WIN_C = 16
SW_HEADS = 6
SW_KV_HEADS = 2
SW_GROUP = SW_HEADS // SW_KV_HEADS
SW_WINDOW = 128
DF_HEADS = 4
DF_QK_DIM = 64
DF_V_DIM = 128
IN_WIDTH = 5120
D_FF = 5632
N_EXPERTS = 8
ROPE_THETA = 10000.0
NORM_EPS = 1e-6
NEG = -1e30

LANES = 128
VMEM_LIMIT = 56 * 1024 * 1024

COL_QA, COL_KA, COL_VA = 0, 6, 12
COL_QB, COL_KB, COL_VB = 18, 24, 26
COL_QC, COL_KC, COL_VC = 28, 32, 36

PROJ_CHUNK = 256
T_PLAIN, T_NORM, T_NORM_ROPE_B, T_NORM_ROPE_C = 0, 1, 2, 3
CHUNK_TYPES = ((T_NORM,) * 6 + (T_PLAIN,) * 3 + (T_NORM_ROPE_B,) * 4 + (T_PLAIN,)
               + (T_NORM_ROPE_C,) * 4 + (T_PLAIN,) * 2)

NA_QROWS = 4
NA_KROWS = 12
NA_TQ = NA_QROWS * GRID_W
NA_TK = NA_KROWS * GRID_W
NA_TILES = 4
SW_TQ = 256
SW_TK = SW_TQ + 2 * SW_WINDOW
SW_TILES = 2
DF_TQ = 256
DF_TK = 512
DF_TILES = 2


def _cparams(sem):
    return pltpu.CompilerParams(dimension_semantics=sem, vmem_limit_bytes=VMEM_LIMIT)


def _dot(a, b):
    return jnp.dot(a, b, preferred_element_type=F32)


def _dot_nt(a, b):
    return lax.dot_general(a, b, (((1,), (1,)), ((), ())), preferred_element_type=F32)


def _ada_kernel(c_ref, w_ref, b_ref, o_ref):
    c = c_ref[...]
    sc = (c * jax.nn.sigmoid(c)).astype(BF16)
    o_ref[0] = _dot(sc, w_ref[0].astype(BF16)) + b_ref[0]


def _ada_call(c_rows, ada_w, ada_b):
    depth, d, n = ada_w.shape
    rows = c_rows.shape[0]
    tn = 1024
    return pl.pallas_call(
        _ada_kernel,
        grid=(depth, n // tn),
        in_specs=[pl.BlockSpec((rows, d), lambda l, j: (0, 0)),
                  pl.BlockSpec((1, d, tn), lambda l, j: (l, 0, j)),
                  pl.BlockSpec((1, 1, tn), lambda l, j: (l, 0, j))],
        out_specs=pl.BlockSpec((1, rows, tn), lambda l, j: (l, 0, j)),
        out_shape=jax.ShapeDtypeStruct((depth, rows, n), F32),
        compiler_params=_cparams(("arbitrary", "arbitrary")),
        name="ada",
    )(c_rows, ada_w, ada_b.reshape(depth, 1, n))


def _rms(v, gain):
    ms = jnp.mean(v * v, axis=-1, keepdims=True)
    return v * lax.rsqrt(ms + NORM_EPS) * gain


def _rms_half(v, gain, lane):
    lo = lane < DF_QK_DIM
    sq = v * v
    s_lo = jnp.sum(jnp.where(lo, sq, 0.0), axis=-1, keepdims=True)
    s_hi = jnp.sum(jnp.where(lo, 0.0, sq), axis=-1, keepdims=True)
    ms = jnp.where(lo, s_lo, s_hi) * (1.0 / DF_QK_DIM)
    return v * lax.rsqrt(ms + NORM_EPS) * gain


def _rope(v, cos, sin, half, lane):
    fwd = pltpu.roll(v, LANES - half, axis=1)
    bwd = pltpu.roll(v, half, axis=1)
    partner = jnp.where((lane & half) == 0, fwd, bwd)
    return v * cos + partner * sin


def _proj_kernel(x_ref, shift_ref, scale_ref, g_ref, w_ref, gain_ref,
                 cb_ref, sb_ref, cc_ref, sc_ref, o_ref):
    h = (_rms(x_ref[...], g_ref[...]) * (1.0 + scale_ref[0]) + shift_ref[0]).astype(BF16)
    lane = lax.broadcasted_iota(jnp.int32, (h.shape[0], LANES), 1)
    for c, t in enumerate(CHUNK_TYPES):
        p = _dot(h, w_ref[:, c * PROJ_CHUNK:(c + 1) * PROJ_CHUNK])
        for hh in range(PROJ_CHUNK // LANES):
            c0 = c * PROJ_CHUNK + hh * LANES
            v = p[:, hh * LANES:(hh + 1) * LANES]
            gain = gain_ref[:, c0:c0 + LANES]
            if t == T_NORM:
                v = _rms(v, gain)
            elif t == T_NORM_ROPE_B:
                v = _rope(_rms(v, gain), cb_ref[...], sb_ref[...], HEAD_DIM // 4, lane)
            elif t == T_NORM_ROPE_C:
                v = _rope(_rms_half(v, gain, lane), cc_ref[...], sc_ref[...], DF_QK_DIM // 4, lane)
            o_ref[:, c0:c0 + LANES] = v.astype(BF16)


def _proj_call(x2d, shift, scale, g, w, gains, tabs, seq, tm):
    t_rows, d = x2d.shape
    n = w.shape[1]
    assert n == PROJ_CHUNK * len(CHUNK_TYPES)
    tpb = seq // tm
    return pl.pallas_call(
        _proj_kernel,
        grid=(t_rows // tm,),
        in_specs=[pl.BlockSpec((tm, d), lambda i: (i, 0)),
                  pl.BlockSpec((1, 1, d), lambda i: (i // tpb, 0, 0)),
                  pl.BlockSpec((1, 1, d), lambda i: (i // tpb, 0, 0)),
                  pl.BlockSpec((1, d), lambda i: (0, 0)),
                  pl.BlockSpec((d, n), lambda i: (0, 0), pipeline_mode=pl.Buffered(1)),
                  pl.BlockSpec((1, n), lambda i: (0, 0))]
                 + [pl.BlockSpec((tm, LANES), lambda i: (i % tpb, 0))] * 4,
        out_specs=pl.BlockSpec((tm, n), lambda i: (i, 0)),
        out_shape=jax.ShapeDtypeStruct((t_rows, n), BF16),
        compiler_params=_cparams(("arbitrary",)),
        name="proj",
    )(x2d, shift, scale, g, w, gains, *tabs)


def _softmax_parts(parts):
    m = parts[0].max(axis=-1, keepdims=True)
    for s in parts[1:]:
        m = jnp.maximum(m, s.max(axis=-1, keepdims=True))
    return m


def _na_kernel(q_ref, k_ref, v_ref, kc_ref, vc_ref, bias_ref, o_ref, *, nblk):
    scale = HEAD_DIM ** -0.5

    def one_block(t):
        blk = pl.program_id(2) * NA_TILES + t
        u0 = jnp.clip(NA_QROWS * blk - NA_WIN_R // 2, 0, GRID_W - NA_KROWS)
        start = pl.multiple_of(u0 * GRID_W, GRID_W)
        var = jnp.where(blk == 0, 0, jnp.where(blk == nblk - 1, 2, 1))
        q = q_ref[0, t * NA_TQ:(t + 1) * NA_TQ, :]
        kw = k_ref[0, pl.ds(start, NA_TK), :]
        vw = v_ref[0, pl.ds(start, NA_TK), :]
        s_nb = _dot_nt(q, kw) * scale + bias_ref[0, var]
        s_cx = _dot_nt(q, kc_ref[0]) * scale
        m = jnp.maximum(s_nb.max(axis=-1, keepdims=True), s_cx.max(axis=-1, keepdims=True))
        p_nb = jnp.exp(s_nb - m)
        p_cx = jnp.exp(s_cx - m)
        l = p_nb.sum(axis=-1, keepdims=True) + p_cx.sum(axis=-1, keepdims=True)
        return (_dot(p_nb.astype(BF16), vw) + _dot(p_cx.astype(BF16), vc_ref[0])) / l

    o_ref[0] = jnp.concatenate([one_block(t) for t in range(NA_TILES)], axis=0).astype(BF16)


def _na_call(p, pc, bias):
    b, seq, _ = p.shape
    lc = pc.shape[1]
    nblk = seq // NA_TQ
    bq = NA_TILES * NA_TQ
    return pl.pallas_call(
        functools.partial(_na_kernel, nblk=nblk),
        grid=(b, NA_HEADS, seq // bq),
        in_specs=[pl.BlockSpec((1, bq, LANES), lambda bi, h, i: (bi, i, COL_QA + h)),
                  pl.BlockSpec((1, seq, LANES), lambda bi, h, i: (bi, 0, COL_KA + h)),
                  pl.BlockSpec((1, seq, LANES), lambda bi, h, i: (bi, 0, COL_VA + h)),
                  pl.BlockSpec((1, lc, LANES), lambda bi, h, i: (bi, 0, COL_KA + h)),
                  pl.BlockSpec((1, lc, LANES), lambda bi, h, i: (bi, 0, COL_VA + h)),
                  pl.BlockSpec((1, 3, NA_TQ, NA_TK), lambda bi, h, i: (h, 0, 0, 0))],
        out_specs=pl.BlockSpec((1, bq, LANES), lambda bi, h, i: (bi, i, h)),
        out_shape=jax.ShapeDtypeStruct((b, seq, NA_HEADS * HEAD_DIM), BF16),
        compiler_params=_cparams(("arbitrary", "arbitrary", "arbitrary")),
        name="na_attn",
    )(p, p, p, pc, pc, bias)


def _sw_kernel(sink_ref, q_ref, k_ref, v_ref, kc_ref, vc_ref, mask_ref, o_ref, *, nblk, seq):
    hkv = pl.program_id(1)
    scale = HEAD_DIM ** -0.5
    kc = kc_ref[0]
    vc = vc_ref[0]

    def one_block(t):
        blk = pl.program_id(2) * SW_TILES + t
        start = pl.multiple_of(jnp.clip(blk * SW_TQ - SW_WINDOW, 0, seq - SW_TK), SW_WINDOW)
        var = jnp.where(blk == 0, 0, jnp.where(blk == nblk - 1, 2, 1))
        kw = k_ref[0, pl.ds(start, SW_TK), :]
        vw = v_ref[0, pl.ds(start, SW_TK), :]
        mask = mask_ref[var]
        outs = []
        for g in range(SW_GROUP):
            q = q_ref[0, t * SW_TQ:(t + 1) * SW_TQ, g * LANES:(g + 1) * LANES]
            sink = sink_ref[hkv * SW_GROUP + g]
            s_w = _dot_nt(q, kw) * scale + mask
            s_c = _dot_nt(q, kc) * scale
            m = jnp.maximum(s_w.max(axis=-1, keepdims=True), s_c.max(axis=-1, keepdims=True))
            m = jnp.maximum(m, sink)
            p_w = jnp.exp(s_w - m)
            p_c = jnp.exp(s_c - m)
            l = p_w.sum(axis=-1, keepdims=True) + p_c.sum(axis=-1, keepdims=True) + jnp.exp(sink - m)
            o = _dot(p_w.astype(BF16), vw) + _dot(p_c.astype(BF16), vc)
            outs.append(o / l)
        return jnp.concatenate(outs, axis=1)

    o_ref[0] = jnp.concatenate([one_block(t) for t in range(SW_TILES)], axis=0).astype(BF16)


def _sw_call(p, pc, sink, mask):
    b, seq, _ = p.shape
    lc = pc.shape[1]
    nblk = seq // SW_TQ
    gw = SW_GROUP * LANES
    bq = SW_TILES * SW_TQ
    grid_spec = pltpu.PrefetchScalarGridSpec(
        num_scalar_prefetch=1,
        grid=(b, SW_KV_HEADS, seq // bq),
        in_specs=[pl.BlockSpec((1, bq, gw), lambda bi, h, i, s: (bi, i, COL_QB // SW_GROUP + h)),
                  pl.BlockSpec((1, seq, LANES), lambda bi, h, i, s: (bi, 0, COL_KB + h)),
                  pl.BlockSpec((1, seq, LANES), lambda bi, h, i, s: (bi, 0, COL_VB + h)),
                  pl.BlockSpec((1, lc, LANES), lambda bi, h, i, s: (bi, 0, COL_KB + h)),
                  pl.BlockSpec((1, lc, LANES), lambda bi, h, i, s: (bi, 0, COL_VB + h)),
                  pl.BlockSpec((3, SW_TQ, SW_TK), lambda bi, h, i, s: (0, 0, 0))],
        out_specs=pl.BlockSpec((1, bq, gw), lambda bi, h, i, s: (bi, i, h)))
    return pl.pallas_call(
        functools.partial(_sw_kernel, nblk=nblk, seq=seq),
        grid_spec=grid_spec,
        out_shape=jax.ShapeDtypeStruct((b, seq, SW_HEADS * HEAD_DIM), BF16),
        compiler_params=_cparams(("arbitrary", "arbitrary", "arbitrary")),
        name="sw_attn",
    )(sink, p, p, p, pc, pc, mask)


def _df_lambda(lam_ref, lam_init):
    lp = lam_ref[0]
    s1 = jnp.sum(jnp.sum(lp[0:1] * lp[1:2], axis=-1, keepdims=True), axis=0, keepdims=True)
    s2 = jnp.sum(jnp.sum(lp[2:3] * lp[3:4], axis=-1, keepdims=True), axis=0, keepdims=True)
    return jnp.exp(s1) - jnp.exp(s2) + lam_init


def _df_kernel(lam_ref, g_ref, q_ref, *refs, lam_init, with_latent, tq):
    if with_latent:
        k_ref, v_ref, kc_ref, vc_ref, o_ref = refs
    else:
        kc_ref, vc_ref, o_ref = refs
    chunks = []
    if with_latent:
        for c in range(k_ref.shape[1] // DF_TK):
            chunks.append((k_ref[0, c * DF_TK:(c + 1) * DF_TK, :], v_ref[0, c * DF_TK:(c + 1) * DF_TK, :]))
    chunks.append((kc_ref[0], vc_ref[0]))
    max_chunks = [kc_ref[0]]
    if with_latent:
        max_chunks += [k_ref[0, c * 2 * DF_TK:(c + 1) * 2 * DF_TK, :] for c in range(k_ref.shape[1] // (2 * DF_TK))]
    lam = _df_lambda(lam_ref, lam_init)

    def lane_fold(x, op, acc=None):
        for j in range(x.shape[1] // LANES):
            blk = x[:, j * LANES:(j + 1) * LANES]
            acc = blk if acc is None else op(acc, blk)
        return acc

    def one_tile(q):
        lane = lax.broadcasted_iota(jnp.int32, q.shape, 1)
        zero = jnp.zeros_like(q)
        q2 = jnp.concatenate([jnp.where(lane < DF_QK_DIM, q, zero),
                              jnp.where(lane < DF_QK_DIM, zero, q)], axis=0)
        m_part = None
        for kb in max_chunks:
            m_part = lane_fold(_dot_nt(q2, kb), jnp.maximum, m_part)
        m = m_part.max(axis=-1, keepdims=True)
        l_part = None
        acc = jnp.zeros((2 * tq, DF_V_DIM), F32)
        for kb, vb in chunks:
            pr = jnp.exp(_dot_nt(q2, kb) - m)
            l_part = lane_fold(pr, jnp.add, l_part)
            acc = acc + _dot(pr.astype(BF16), vb)
        o2 = acc / l_part.sum(axis=-1, keepdims=True)
        o = o2[:tq] - lam * o2[tq:]
        return _rms(o, g_ref[...]) * (1.0 - lam_init)

    outs = [one_tile(q_ref[0, t * tq:(t + 1) * tq, :]) for t in range(q_ref.shape[1] // tq)]
    o_ref[0] = jnp.concatenate(outs, axis=0).astype(BF16)


def _df_call(pq, pk, pc, lam_p, subln_g, lam_init, with_latent):
    b, lq, _ = pq.shape
    lc = pc.shape[1]
    tq = min(DF_TQ, lq)
    bq = min(DF_TILES * tq, lq)
    in_specs = [pl.BlockSpec((1, 4, DF_QK_DIM), lambda bi, h, i: (0, 0, 0)),
                pl.BlockSpec((1, DF_V_DIM), lambda bi, h, i: (0, 0)),
                pl.BlockSpec((1, bq, LANES), lambda bi, h, i: (bi, i, COL_QC + h))]
    args = [lam_p, subln_g, pq]
    if with_latent:
        seq = pk.shape[1]
        in_specs += [pl.BlockSpec((1, seq, LANES), lambda bi, h, i: (bi, 0, COL_KC + h)),
                     pl.BlockSpec((1, seq, LANES), lambda bi, h, i: (bi, 0, COL_VC + h))]
        args += [pk, pk]
    in_specs += [pl.BlockSpec((1, lc, LANES), lambda bi, h, i: (bi, 0, COL_KC + h)),
                 pl.BlockSpec((1, lc, LANES), lambda bi, h, i: (bi, 0, COL_VC + h))]
    args += [pc, pc]
    return pl.pallas_call(
        functools.partial(_df_kernel, lam_init=lam_init, with_latent=with_latent, tq=tq),
        grid=(b, DF_HEADS, lq // bq),
        in_specs=in_specs,
        out_specs=pl.BlockSpec((1, bq, LANES), lambda bi, h, i: (bi, i, h)),
        out_shape=jax.ShapeDtypeStruct((b, lq, DF_HEADS * DF_V_DIM), BF16),
        compiler_params=_cparams(("arbitrary", "arbitrary", "arbitrary")),
        name="df_attn" if with_latent else "df_attn_ctx",
    )(*args)


def _ctx_kernel(sink_ref, q_ref, k_ref, v_ref, o_ref, *, group, has_sink):
    hkv = pl.program_id(1)
    scale = HEAD_DIM ** -0.5
    k = k_ref[0]
    v = v_ref[0]
    for g in range(group):
        q = q_ref[0, :, g * LANES:(g + 1) * LANES]
        s = _dot_nt(q, k) * scale
        m = s.max(axis=-1, keepdims=True)
        if has_sink:
            sink = sink_ref[hkv * group + g]
            m = jnp.maximum(m, sink)
        pr = jnp.exp(s - m)
        l = pr.sum(axis=-1, keepdims=True)
        if has_sink:
            l = l + jnp.exp(sink - m)
        o_ref[0, :, g * LANES:(g + 1) * LANES] = (_dot(pr.astype(BF16), v) / l).astype(BF16)


def _ctx_call(pc, sink, col_q, col_k, col_v, n_kv, group, has_sink):
    b, lc, _ = pc.shape
    gw = group * LANES
    grid_spec = pltpu.PrefetchScalarGridSpec(
        num_scalar_prefetch=1,
        grid=(b, n_kv),
        in_specs=[pl.BlockSpec((1, lc, gw), lambda bi, h, s: (bi, 0, col_q // group + h)),
                  pl.BlockSpec((1, lc, LANES), lambda bi, h, s: (bi, 0, col_k + h)),
                  pl.BlockSpec((1, lc, LANES), lambda bi, h, s: (bi, 0, col_v + h))],
        out_specs=pl.BlockSpec((1, lc, gw), lambda bi, h, s: (bi, 0, h)))
    return pl.pallas_call(
        functools.partial(_ctx_kernel, group=group, has_sink=has_sink),
        grid_spec=grid_spec,
        out_shape=jax.ShapeDtypeStruct((b, lc, n_kv * gw), BF16),
        compiler_params=_cparams(("arbitrary", "arbitrary")),
        name="ctx_attn",
    )(sink, pc, pc, pc)


def _outproj_kernel(oa_ref, ob_ref, oc_ref, wa_ref, wb_ref, wc_ref, x_ref, gate_ref, g_ref,
                    shift_ref, scale_ref, *refs, with_router):
    if with_router:
        rh_ref, rl_ref, xo_ref, ho_ref, go_ref = refs
    else:
        xo_ref, ho_ref = refs
    y = _dot(oa_ref[...], wa_ref[...]) + _dot(ob_ref[...], wb_ref[...]) + _dot(oc_ref[...], wc_ref[...])
    xn = x_ref[...] + gate_ref[0] * y
    xo_ref[...] = xn
    h = _rms(xn, g_ref[...]) * (1.0 + scale_ref[0]) + shift_ref[0]
    hb = h.astype(BF16)
    ho_ref[...] = h.astype(ho_ref.dtype)
    if with_router:
        hl = (h - hb.astype(F32)).astype(BF16)
        logits = _dot(hb, rh_ref[...]) + (_dot(hb, rl_ref[...]) + _dot(hl, rh_ref[...]))
        lane = lax.broadcasted_iota(jnp.int32, logits.shape, 1).astype(F32)
        logits = jnp.where(lane < N_EXPERTS, logits, NEG)
        v1 = logits.max(axis=-1, keepdims=True)
        i1 = jnp.where(logits == v1, lane, float(LANES)).min(axis=-1, keepdims=True)
        rest = jnp.where(lane == i1, NEG, logits)
        v2 = rest.max(axis=-1, keepdims=True)
        i2 = jnp.where(rest == v2, lane, float(LANES)).min(axis=-1, keepdims=True)
        e2 = jnp.exp(v2 - v1)
        den = 1.0 + e2
        go_ref[...] = (jnp.where(lane == 0.0, i1, 0.0) + jnp.where(lane == 1.0, i2, 0.0)
                       + jnp.where(lane == 2.0, 1.0 / den, 0.0) + jnp.where(lane == 3.0, e2 / den, 0.0))


def _outproj_call(oa, ob, oc, w_out, x2d, gate, g2, shift, scale, seq, tm, router=None):
    t_rows, d = x2d.shape
    tpb = seq // tm
    wa, wb, wc = oa.shape[1], ob.shape[1], oc.shape[1]
    assert wa == wb and (wa + wb) % wc == 0
    row = lambda i: (i, 0)
    fixed = lambda i: (0, 0)
    per_b = lambda i: (i // tpb, 0, 0)
    in_specs = [pl.BlockSpec((tm, wa), row), pl.BlockSpec((tm, wb), row), pl.BlockSpec((tm, wc), row),
                pl.BlockSpec((wa, d), lambda i: (0, 0)), pl.BlockSpec((wb, d), lambda i: (1, 0)),
                pl.BlockSpec((wc, d), lambda i: ((wa + wb) // wc, 0)),
                pl.BlockSpec((tm, d), row), pl.BlockSpec((1, 1, d), per_b), pl.BlockSpec((1, d), fixed),
                pl.BlockSpec((1, 1, d), per_b), pl.BlockSpec((1, 1, d), per_b)]
    args = [oa, ob, oc, w_out, w_out, w_out, x2d, gate, g2, shift, scale]
    out_specs = [pl.BlockSpec((tm, d), row), pl.BlockSpec((tm, d), row)]
    out_shape = [jax.ShapeDtypeStruct((t_rows, d), F32),
                 jax.ShapeDtypeStruct((t_rows, d), BF16 if router is None else F32)]
    if router is not None:
        in_specs += [pl.BlockSpec((d, LANES), fixed), pl.BlockSpec((d, LANES), fixed)]
        args += list(router)
        out_specs.append(pl.BlockSpec((tm, LANES), row))
        out_shape.append(jax.ShapeDtypeStruct((t_rows, LANES), F32))
    return pl.pallas_call(
        functools.partial(_outproj_kernel, with_router=router is not None),
        grid=(t_rows // tm,),
        in_specs=in_specs, out_specs=out_specs, out_shape=out_shape,
        compiler_params=_cparams(("arbitrary",)),
        name="outproj",
    )(*args)


def _swiglu_step(h, w1_ref, w3_ref, w2_ref, acc_scr):
    a = _dot(h, w1_ref[...])
    b = _dot(h, w3_ref[...])
    act = (a * jax.nn.sigmoid(a) * b).astype(BF16)
    acc_scr[...] += _dot(act, w2_ref[...])


def _ffn_kernel(h_ref, w1_ref, w3_ref, w2_ref, x_ref, gate_ref, o_ref):
    f = pl.program_id(1)

    @pl.when(f == 0)
    def _():
        o_ref[...] = jnp.zeros(o_ref.shape, F32)

    _swiglu_step(h_ref[...], w1_ref, w3_ref, w2_ref, o_ref)

    @pl.when(f == pl.num_programs(1) - 1)
    def _():
        o_ref[...] = x_ref[...] + gate_ref[0] * o_ref[...]


def _ffn_call(h, w1, w3, w2, x2d, gate, seq, tm, tf=512):
    t_rows, d = x2d.shape
    ff = w1.shape[1]
    tpb = seq // tm
    return pl.pallas_call(
        _ffn_kernel,
        grid=(t_rows // tm, ff // tf),
        in_specs=[pl.BlockSpec((tm, d), lambda i, f: (i, 0)),
                  pl.BlockSpec((d, tf), lambda i, f: (0, f)),
                  pl.BlockSpec((d, tf), lambda i, f: (0, f)),
                  pl.BlockSpec((tf, d), lambda i, f: (f, 0)),
                  pl.BlockSpec((tm, d), lambda i, f: (i, 0)),
                  pl.BlockSpec((1, 1, d), lambda i, f: (i // tpb, 0, 0))],
        out_specs=pl.BlockSpec((tm, d), lambda i, f: (i, 0)),
        out_shape=jax.ShapeDtypeStruct((t_rows, d), F32),
        compiler_params=_cparams(("arbitrary", "arbitrary")),
        name="ffn",
    )(h, w1, w3, w2, x2d, gate)


MOE_TM = 512
ROW_UNROLL = 8


def _route_plan(info, tm):
    t_rows = info.shape[0]
    e_flat = info[:, :2].astype(jnp.int32).T.reshape(-1)
    onehot = (e_flat[:, None] == jnp.arange(N_EXPERTS, dtype=jnp.int32)[None, :]).astype(jnp.int32)
    csum = jnp.cumsum(onehot, axis=0)
    counts = csum[-1]
    padded = ((counts + tm - 1) // tm) * tm
    pend = jnp.cumsum(padded)
    pstart = pend - padded
    dest = jnp.sum(onehot * (csum - 1 + pstart[None, :]), axis=1).astype(jnp.int32)
    n_tiles = (2 * t_rows) // tm + N_EXPERTS
    n_used = (pend[-1] // tm).astype(jnp.int32)
    tile_idx = jnp.arange(n_tiles, dtype=jnp.int32)
    owner = jnp.sum((tile_idx[:, None] * tm >= pend[None, :]).astype(jnp.int32), axis=1)
    last_owner = jnp.sum(((n_used - 1) * tm >= pend).astype(jnp.int32))
    tile_expert = jnp.where(tile_idx < n_used, owner, last_owner).astype(jnp.int32)
    return dest, tile_expert, n_used.reshape(1)


def _row_copy_loop(n_rows, start_one):
    def issue(r, carry):
        for cp in start_one(r):
            cp.start()
        return carry
    lax.fori_loop(0, n_rows, issue, 0, unroll=ROW_UNROLL)

    def drain(r, carry):
        for cp in start_one(r):
            cp.wait()
        return carry
    lax.fori_loop(0, n_rows, drain, 0, unroll=ROW_UNROLL)


def _scatter_kernel(dest_ref, h_ref, xg_in, xg_out, sem, *, ts, t_rows):
    del xg_in
    base = pl.program_id(0) * ts

    def copies(r):
        t = base + r
        return [pltpu.make_async_copy(h_ref.at[pl.ds(r, 1)], xg_out.at[pl.ds(dest_ref[k * t_rows + t], 1)], sem)
                for k in range(2)]
    _row_copy_loop(ts, copies)


def _scatter_call(dest, h, n_rows, ts=256):
    t_rows, d = h.shape
    grid_spec = pltpu.PrefetchScalarGridSpec(
        num_scalar_prefetch=1, grid=(t_rows // ts,),
        in_specs=[pl.BlockSpec((ts, d), lambda i, dst: (i, 0)), pl.BlockSpec(memory_space=pl.ANY)],
        out_specs=pl.BlockSpec(memory_space=pl.ANY),
        scratch_shapes=[pltpu.SemaphoreType.DMA(())])
    return pl.pallas_call(
        functools.partial(_scatter_kernel, ts=ts, t_rows=t_rows),
        grid_spec=grid_spec,
        out_shape=jax.ShapeDtypeStruct((n_rows, d), h.dtype),
        input_output_aliases={2: 0},
        compiler_params=_cparams(("arbitrary",)),
        name="moe_scatter",
    )(dest, h, jnp.zeros((n_rows, d), h.dtype))


def _moe_kernel(te_ref, nu_ref, xg_ref, w1_ref, w3_ref, w2_ref, y_ref, hb_scr):
    del te_ref
    i = pl.program_id(0)
    f = pl.program_id(1)

    @pl.when(f == 0)
    def _():
        y_ref[...] = jnp.zeros(y_ref.shape, F32)

    @pl.when(i < nu_ref[0])
    def _():
        @pl.when(f == 0)
        def _():
            hb_scr[...] = xg_ref[...].astype(BF16)

        _swiglu_step(hb_scr[...], w1_ref.at[0], w3_ref.at[0], w2_ref.at[0], y_ref)


def _moe_call(tile_expert, n_used, xg, w1, w3, w2, tm, tf=512):
    n_rows, d = xg.shape
    ff = w1.shape[2]
    nf = ff // tf
    f_eff = lambda i, f, nu: jnp.where(i < nu[0], f, nf - 1)
    grid_spec = pltpu.PrefetchScalarGridSpec(
        num_scalar_prefetch=2, grid=(n_rows // tm, nf),
        in_specs=[pl.BlockSpec((tm, d), lambda i, f, te, nu: (i, 0)),
                  pl.BlockSpec((1, d, tf), lambda i, f, te, nu: (te[i], 0, f_eff(i, f, nu))),
                  pl.BlockSpec((1, d, tf), lambda i, f, te, nu: (te[i], 0, f_eff(i, f, nu))),
                  pl.BlockSpec((1, tf, d), lambda i, f, te, nu: (te[i], f_eff(i, f, nu), 0))],
        out_specs=pl.BlockSpec((tm, d), lambda i, f, te, nu: (i, 0)),
        scratch_shapes=[pltpu.VMEM((tm, d), BF16)])
    return pl.pallas_call(
        _moe_kernel,
        grid_spec=grid_spec,
        out_shape=jax.ShapeDtypeStruct((n_rows, d), F32),
        compiler_params=_cparams(("arbitrary", "arbitrary")),
        name="moe_ffn",
    )(tile_expert, n_used, xg, w1, w3, w2)


def _combine_kernel(dest_ref, info_ref, x_ref, gate_ref, y_hbm, o_ref, buf, sem, *, tc, t_rows):
    base = pl.program_id(0) * tc

    def copies(r):
        t = base + r
        return [pltpu.make_async_copy(y_hbm.at[pl.ds(dest_ref[k * t_rows + t], 1)], buf.at[k, pl.ds(r, 1)], sem)
                for k in range(2)]
    _row_copy_loop(tc, copies)
    info = info_ref[...]
    y = info[:, 2:3] * buf[0] + info[:, 3:4] * buf[1]
    o_ref[...] = x_ref[...] + gate_ref[0] * y


def _combine_call(dest, info, x2d, gate, y, seq, tc=256):
    t_rows, d = x2d.shape
    tpb = seq // tc
    grid_spec = pltpu.PrefetchScalarGridSpec(
        num_scalar_prefetch=1, grid=(t_rows // tc,),
        in_specs=[pl.BlockSpec((tc, LANES), lambda i, dst: (i, 0)),
                  pl.BlockSpec((tc, d), lambda i, dst: (i, 0)),
                  pl.BlockSpec((1, 1, d), lambda i, dst: (i // tpb, 0, 0)),
                  pl.BlockSpec(memory_space=pl.ANY)],
        out_specs=pl.BlockSpec((tc, d), lambda i, dst: (i, 0)),
        scratch_shapes=[pltpu.VMEM((2, tc, d), F32), pltpu.SemaphoreType.DMA(())])
    return pl.pallas_call(
        functools.partial(_combine_kernel, tc=tc, t_rows=t_rows),
        grid_spec=grid_spec,
        out_shape=jax.ShapeDtypeStruct((t_rows, d), F32),
        compiler_params=_cparams(("arbitrary",)),
        name="moe_combine",
    )(dest, info, x2d, gate, y)


def _moe_block(h_f32, info, x2d, gate, w1, w3, w2, seq):
    t_rows = x2d.shape[0]
    dest, tile_expert, n_used = _route_plan(info, MOE_TM)
    n_rows = 2 * t_rows + N_EXPERTS * MOE_TM
    xg = _scatter_call(dest, h_f32, n_rows)
    y = _moe_call(tile_expert, n_used, xg, w1, w3, w2, MOE_TM)
    return _combine_call(dest, info, x2d, gate, y, seq)


def _rope_table(n_tokens, head_dim):
    axis_dim = head_dim // 2
    inv_freq = ROPE_THETA ** (-jnp.arange(0, axis_dim, 2, dtype=F32) / axis_dim)
    t = jnp.arange(n_tokens, dtype=jnp.int32)
    row = (t // GRID_W).astype(F32)
    col = (t % GRID_W).astype(F32)
    ang_r = row[:, None] * inv_freq[None, :]
    ang_c = col[:, None] * inv_freq[None, :]
    cos = jnp.concatenate([jnp.cos(ang_r)] * 2 + [jnp.cos(ang_c)] * 2, axis=-1)
    sin = jnp.concatenate([-jnp.sin(ang_r), jnp.sin(ang_r), -jnp.sin(ang_c), jnp.sin(ang_c)], axis=-1)
    reps = LANES // head_dim
    return jnp.tile(cos, (1, reps)), jnp.tile(sin, (1, reps))


def _identity_rope_table(n_tokens):
    return jnp.ones((n_tokens, LANES), F32), jnp.zeros((n_tokens, LANES), F32)


def _na_bias_selectors(seq):
    rows = seq // GRID_W
    nblk = rows // NA_QROWS
    col = np.arange(GRID_W)
    c0 = np.clip(col - NA_WIN_C // 2, 0, GRID_W - NA_WIN_C)[:, None]
    col_ok = (col[None, :] >= c0) & (col[None, :] < c0 + NA_WIN_C)
    ic = col[None, :] - col[:, None] + NA_WIN_C - 1
    sel_c = (col_ok[..., None] & (ic[..., None] == np.arange(2 * NA_WIN_C - 1))).astype(np.float32)
    sel_r = np.zeros((3, NA_QROWS, NA_KROWS, 2 * NA_WIN_R - 1), np.float32)
    row_ok = np.zeros((3, NA_QROWS, NA_KROWS), bool)
    for v, blk in enumerate((0, 1, nblk - 1)):
        r_base = blk * NA_QROWS
        u0 = int(np.clip(r_base - NA_WIN_R // 2, 0, rows - NA_KROWS))
        for q in range(NA_QROWS):
            r = r_base + q
            r0 = int(np.clip(r - NA_WIN_R // 2, 0, rows - NA_WIN_R))
            for k in range(NA_KROWS):
                kr = u0 + k
                if r0 <= kr < r0 + NA_WIN_R:
                    row_ok[v, q, k] = True
                    sel_r[v, q, k, kr - r + NA_WIN_R - 1] = 1.0
    valid = row_ok[:, :, None, :, None] & col_ok[None, None, :, None, :]
    return sel_r, sel_c, valid.reshape(3, NA_TQ, NA_TK)


def _na_bias_table(rpb, seq):
    sel_r, sel_c, valid = _na_bias_selectors(seq)
    hp = lax.Precision.HIGHEST
    t_col = jnp.einsum('hdi,cxi->hdcx', rpb.astype(F32), sel_c, precision=hp)
    full = jnp.einsum('vqkd,hdcx->hvqckx', sel_r, t_col, precision=hp)
    full = full.reshape(rpb.shape[0], 3, NA_TQ, NA_TK)
    return jnp.where(valid[None], full, NEG)


def _sw_mask_table(seq):
    nblk = seq // SW_TQ
    out = []
    for blk in (0, 1, nblk - 1):
        start = int(np.clip(blk * SW_TQ - SW_WINDOW, 0, seq - SW_TK))
        qpos = (blk * SW_TQ + np.arange(SW_TQ))[:, None]
        kpos = (start + np.arange(SW_TK))[None, :]
        out.append(np.where(np.abs(qpos - kpos) <= SW_WINDOW, 0.0, NEG))
    return jnp.asarray(np.stack(out), F32)


def _col_gains(na_q, na_k, sw_q, sw_k, df_q, df_k):
    one = lambda n: jnp.ones((n,), F32)
    df_scale = DF_QK_DIM ** -0.5
    return jnp.concatenate([
        jnp.tile(na_q, NA_HEADS), jnp.tile(na_k, NA_HEADS), one(NA_HEADS * HEAD_DIM),
        jnp.tile(sw_q, SW_HEADS), jnp.tile(sw_k, SW_KV_HEADS), one(SW_KV_HEADS * HEAD_DIM),
        jnp.tile(df_q * df_scale, 2 * DF_HEADS), jnp.tile(df_k, 2 * DF_HEADS), one(DF_HEADS * DF_V_DIM),
    ]).astype(F32)[None, :]


def kernel(x, c, ctx, c_ctx, ada_w, ada_b, norm1_g, norm2_g, w_in, w_out, na_q_norm, na_k_norm, na_rpb,
           sw_q_norm, sw_k_norm, sw_sink, df_q_norm, df_k_norm, df_lambda, df_subln_g,
           ffn_w1, ffn_w3, ffn_w2, moe_router, moe_w1, moe_w3, moe_w2):
    b, seq, d = x.shape
    lc = ctx.shape[1]
    depth = ada_w.shape[0]
    assert d == D_MODEL and seq % (NA_TILES * NA_TQ) == 0 and seq % (2 * DF_TK) == 0

    c_rows = jnp.concatenate([c, c_ctx[None], jnp.zeros((16 - b - 1, d), F32)], axis=0)
    mod = _ada_call(c_rows, ada_w, ada_b)

    tabs_x = _rope_table(seq, HEAD_DIM) + _rope_table(seq, DF_QK_DIM)
    tabs_c = _identity_rope_table(lc) * 2
    sw_mask = _sw_mask_table(seq)

    x2 = x.reshape(b * seq, d)
    xc2 = ctx.reshape(b * lc, d)
    tm_x = 512
    tm_c = lc

    for i in range(depth):
        last = i == depth - 1
        m = mod[i].reshape(16, 6, d)
        mx = [m[:b, j][:, None, :] for j in range(6)]
        mc = [jnp.broadcast_to(m[b:b + 1, j][:, None, :], (b, 1, d)) for j in range(6)]
        lam_init = 0.8 - 0.6 * math.exp(-0.3 * i)
        gains = _col_gains(na_q_norm[i], na_k_norm[i], sw_q_norm[i], sw_k_norm[i], df_q_norm[i], df_k_norm[i])
        g1 = norm1_g[i][None, :]
        g2 = norm2_g[i][None, :]
        w_in_b = w_in[i].astype(BF16)
        w_out_b = w_out[i].astype(BF16)
        sink = sw_sink[i].astype(F32)
        lam_p = df_lambda[i][None].astype(F32)
        subln = df_subln_g[i][None, :]

        p = _proj_call(x2, mx[0], mx[1], g1, w_in_b, gains, tabs_x, seq, tm_x).reshape(b, seq, IN_WIDTH)
        pc = _proj_call(xc2, mc[0], mc[1], g1, w_in_b, gains, tabs_c, lc, tm_c).reshape(b, lc, IN_WIDTH)
        oa = _na_call(p, pc, _na_bias_table(na_rpb[i], seq))
        ob = _sw_call(p, pc, sink, sw_mask)
        oc = _df_call(p, p, pc, lam_p, subln, lam_init, True)

        j = i // 2
        moe = i % 2 == 1
        if moe:
            r = jnp.pad(moe_router[j].astype(F32), ((0, 0), (0, LANES - N_EXPERTS)))
            r_hi = r.astype(BF16)
            r_lo = (r - r_hi.astype(F32)).astype(BF16)
            router = (r_hi, r_lo)
            w1, w3, w2 = moe_w1[j].astype(BF16), moe_w3[j].astype(BF16), moe_w2[j].astype(BF16)
        else:
            router = None
            w1, w3, w2 = ffn_w1[j].astype(BF16), ffn_w3[j].astype(BF16), ffn_w2[j].astype(BF16)

        def mix_and_ffn(oa_, ob_, oc_, xs, mv, seq_, tm_):
            flat = lambda o: o.reshape(xs.shape[0], o.shape[-1])
            res = _outproj_call(flat(oa_), flat(ob_), flat(oc_), w_out_b, xs, mv[2], g2, mv[3], mv[4],
                                seq_, min(tm_, 256), router)
            if moe:
                return _moe_block(res[1], res[2], res[0], mv[5], w1, w3, w2, seq_)
            return _ffn_call(res[1], w1, w3, w2, res[0], mv[5], seq_, tm_)

        if not last:
            coa = _ctx_call(pc, sink, COL_QA, COL_KA, COL_VA, NA_HEADS, 1, False)
            cob = _ctx_call(pc, sink, COL_QB, COL_KB, COL_VB, SW_KV_HEADS, SW_GROUP, True)
            coc = _df_call(pc, None, pc, lam_p, subln, lam_init, False)
            xc2 = mix_and_ffn(coa, cob, coc, xc2, mc, lc, tm_c)
        x2 = mix_and_ffn(oa, ob, oc, x2, mx, seq, tm_x)
    return x2.reshape(b, seq, d)
```
